```python
import math
import jax, jax.numpy as jnp
from jax import lax
import numpy as np

D_MODEL = 2048
BATCH = 2
SEQ = 4096
DEPTH = 1
DEC_BATCH = 16
DEC_SEQ = 16
PAST_LEN = 1024

CHUNK = 64
N_HEADS = D_MODEL // 256
HEAD_DIM = 128
N_KV_HEADS = N_HEADS // 2
ATT_W = N_HEADS * HEAD_DIM
KV_W = N_KV_HEADS * HEAD_DIM
IDX_HEADS = D_MODEL // 128
IDX_DIM = 64
IDX_TOPK = 256
Q_BLOCK = 128
ROPE_THETA = 10000.0
RWKV_HEAD = 64
RWKV_HEADS = D_MODEL // (2 * RWKV_HEAD)
RWKV_W = RWKV_HEADS * RWKV_HEAD
DECAY_LORA = 96
AAA_LORA = 96
GATE_LORA = 256
RWKV_SIZES = (RWKV_W, RWKV_W, RWKV_W, DECAY_LORA, AAA_LORA, GATE_LORA)
RWKV_IN_W = 3 * RWKV_W + DECAY_LORA + AAA_LORA + GATE_LORA
IN_SIZES = (ATT_W, KV_W, KV_W, IDX_HEADS * IDX_DIM, IDX_DIM, IDX_HEADS, D_MODEL, D_MODEL, RWKV_IN_W)
IN_W = ATT_W + 2 * KV_W + IDX_HEADS * IDX_DIM + IDX_DIM + IDX_HEADS + 2 * D_MODEL + RWKV_IN_W
PEER_HEADS = 8
PEER_KEYS = 128
PEER_EXPERTS = PEER_KEYS * PEER_KEYS
PEER_DK = 256
PEER_TOPK = 16
PEER_BLOCK = 128
NORM_EPS = 1e-6
GN_EPS = 64e-5

kernel_name = "dsa_rwkv7_peer_streaming_step"


def _split(z, sizes):
    cuts = [int(c) for c in np.cumsum(sizes)[:-1]]
    return jnp.split(z, cuts, axis=-1)


def rmsnorm(x, g):
    xf = x.astype(jnp.float32)
    y = xf * lax.rsqrt(jnp.mean(xf * xf, axis=-1, keepdims=True) + NORM_EPS)
    return (y * g.astype(jnp.float32)).astype(x.dtype)


def rope(x, pos):
    d = x.shape[-1]
    inv = jnp.power(ROPE_THETA, -jnp.arange(0, d, 2, dtype=jnp.float32) / d)
    ang = pos.astype(jnp.float32)[:, None] * inv[None, :]
    cos = jnp.cos(ang)[None, :, None, :]
    sin = jnp.sin(ang)[None, :, None, :]
    xf = x.astype(jnp.float32)
    x1, x2 = xf[..., : d // 2], xf[..., d // 2:]
    return jnp.concatenate([x1 * cos - x2 * sin, x2 * cos + x1 * sin], axis=-1).astype(x.dtype)


def dsa_block(q, qi, wi, q_pos, k_all, v_all, ki_all, k_pos, topk):
    B, Q = q.shape[0], q.shape[1]
    rel = jax.nn.relu(jnp.einsum('bqhd,bsd->bqhs', qi.astype(jnp.float32), ki_all.astype(jnp.float32)) * IDX_DIM ** -0.5)
    score = jnp.einsum('bqhs,bqh->bqs', rel, wi.astype(jnp.float32))
    visible = (k_pos[None, :] // CHUNK) <= (q_pos[:, None] // CHUNK)
    score = jnp.where(visible[None], score, -jnp.inf)
    top_val, top_idx = lax.top_k(score, topk)
    valid = jnp.isfinite(top_val)
    gather = jax.vmap(lambda t, i: t[i])
    k_sel = gather(k_all, top_idx)
    v_sel = gather(v_all, top_idx)
    qg = q.reshape(B, Q, N_KV_HEADS, N_HEADS // N_KV_HEADS, HEAD_DIM)
    logits = jnp.einsum('bqkgd,bqskd->bqkgs', qg, k_sel).astype(jnp.float32) * HEAD_DIM ** -0.5
    logits = jnp.where(valid[:, :, None, None, :], logits, -jnp.inf)
    p = jax.nn.softmax(logits, axis=-1).astype(v_sel.dtype)
    o = jnp.einsum('bqkgs,bqskd->bqkgd', p, v_sel)
    return o.reshape(B, Q, ATT_W)


def dsa_attention(q, qi, wi, pos, k_all, v_all, ki_all, k_pos, topk):
    B, T = q.shape[0], q.shape[1]
    qb = Q_BLOCK if T % Q_BLOCK == 0 else T
    nb = T // qb

    def to_blocks(a):
        return jnp.moveaxis(a.reshape((B, nb, qb) + a.shape[2:]), 1, 0)

    def one(args):
        qq, qqi, wwi, pp = args
        return dsa_block(qq, qqi, wwi, pp, k_all, v_all, ki_all, k_pos, topk)

    out = lax.map(one, (to_blocks(q), to_blocks(qi), to_blocks(wi), pos.reshape(nb, qb)))
    return jnp.moveaxis(out, 0, 1).reshape(B, T, ATT_W)


def wkv7_scan(s0, r, w, k, v, a, b):
    def step(s, inp):
        rt, wt, kt, vt, at, bt = inp
        sa = jnp.einsum('bhij,bhj->bhi', s, at)
        s = s * wt[:, :, None, :] + sa[..., None] * bt[:, :, None, :] + vt[..., None] * kt[:, :, None, :]
        return s, jnp.einsum('bhij,bhj->bhi', s, rt)

    tm = lambda t: jnp.moveaxis(t, 1, 0)
    s_T, ys = lax.scan(step, s0, (tm(r), tm(w), tm(k), tm(v), tm(a), tm(b)))
    return jnp.moveaxis(ys, 0, 1), s_T


def rwkv7_branch(z, shift0, s0, p):
    B, T, _ = z.shape
    f32 = jnp.float32
    prev = jnp.concatenate([shift0, z[:, :-1]], axis=1)
    zm = (z + (prev - z) * p['rwkv_mu']).astype(f32)
    r, k, v, zw, za, zg = _split(zm, RWKV_SIZES)
    w_log = -jax.nn.softplus(-(p['rwkv_w0'].astype(f32) + jnp.tanh(zw) @ p['rwkv_w2'].astype(f32))) - 0.5
    decay = jnp.exp(-jnp.exp(w_log))
    a = jax.nn.sigmoid(p['rwkv_a0'].astype(f32) + za @ p['rwkv_a2'].astype(f32))
    g = jax.nn.sigmoid(zg) @ p['rwkv_g2'].astype(f32)
    hs = lambda t: t.reshape(B, T, RWKV_HEADS, RWKV_HEAD)
    kk = hs(k * p['rwkv_k_k'].astype(f32))
    kk = kk / jnp.maximum(jnp.sqrt(jnp.sum(kk * kk, axis=-1, keepdims=True)), 1e-12)
    k = k * (1.0 + (a - 1.0) * p['rwkv_k_a'].astype(f32))
    r, k, v, decay, a = hs(r), hs(k), hs(v), hs(decay), hs(a)
    y, s_T = wkv7_scan(s0.astype(f32), r, decay, k, v, -kk, kk * a)
    mean = jnp.mean(y, axis=-1, keepdims=True)
    var = jnp.mean(jnp.square(y - mean), axis=-1, keepdims=True)
    y = ((y - mean) * lax.rsqrt(var + GN_EPS)).reshape(B, T, RWKV_W)
    y = y * p['rwkv_ln_w'].astype(f32) + p['rwkv_ln_b'].astype(f32)
    bonus = jnp.sum(r * k * p['rwkv_r_k'].astype(f32), axis=-1, keepdims=True) * v
    y = (y + bonus.reshape(B, T, RWKV_W)) * g
    return y.astype(z.dtype), s_T.astype(s0.dtype), z[:, -1:]


def peer_ffn(x, p):
    B, T, D = x.shape
    n = B * T
    npad = -(-n // PEER_BLOCK) * PEER_BLOCK
    xf = jnp.pad(x.reshape(n, D), ((0, npad - n), (0, 0)))
    wq, subkeys, u_tab, v_tab = p['peer_wq'], p['peer_subkeys'], p['peer_u'], p['peer_v']

    def one(xb):
        P = xb.shape[0]
        q = (xb @ wq).reshape(P, PEER_HEADS, 2, PEER_DK // 2)
        s1 = jnp.einsum('phd,hnd->phn', q[:, :, 0], subkeys[0]).astype(jnp.float32)
        s2 = jnp.einsum('phd,hnd->phn', q[:, :, 1], subkeys[1]).astype(jnp.float32)
        v1, i1 = lax.top_k(s1, PEER_TOPK)
        v2, i2 = lax.top_k(s2, PEER_TOPK)
        cand = (v1[..., :, None] + v2[..., None, :]).reshape(P, PEER_HEADS, PEER_TOPK * PEER_TOPK)
        cidx = (i1[..., :, None] * PEER_KEYS + i2[..., None, :]).reshape(P, PEER_HEADS, PEER_TOPK * PEER_TOPK)
        sv, si = lax.top_k(cand, PEER_TOPK)
        e = jnp.take_along_axis(cidx, si, axis=-1)
        gate = jax.nn.softmax(sv, axis=-1)
        u = u_tab[e]
        v = v_tab[e]
        act = jax.nn.gelu(jnp.einsum('phkd,pd->phk', u, xb).astype(jnp.float32), approximate=False)
        return jnp.einsum('phk,phkd->pd', (gate * act).astype(v.dtype), v)

    out = lax.map(one, xf.reshape(npad // PEER_BLOCK, PEER_BLOCK, D))
    return out.reshape(npad, D)[:n].reshape(B, T, D)


def trunk_layer(x, pos, past_k, past_v, past_ki, past_pos, s0, shift0, p):
    B, T, _ = x.shape
    xn = rmsnorm(x, p['norm1_g'])
    z = xn @ p['w_in']
    q, k, v, qi, ki, wi, ga, gb, zr = _split(z, IN_SIZES)
    q = rope(q.reshape(B, T, N_HEADS, HEAD_DIM), pos)
    k = rope(k.reshape(B, T, N_KV_HEADS, HEAD_DIM), pos)
    v = v.reshape(B, T, N_KV_HEADS, HEAD_DIM)
    qi = rope(qi.reshape(B, T, IDX_HEADS, IDX_DIM), pos)
    ki = rope(ki[:, :, None, :], pos)[:, :, 0]
    wi = wi * IDX_HEADS ** -0.5
    k_all = jnp.concatenate([past_k, k], axis=1)
    v_all = jnp.concatenate([past_v, v], axis=1)
    ki_all = jnp.concatenate([past_ki, ki], axis=1)
    k_pos = jnp.concatenate([past_pos, pos])
    topk = min(IDX_TOPK, k_all.shape[1] // 4)
    att = dsa_attention(q, qi, wi, pos, k_all, v_all, ki_all, k_pos, topk)
    rw, s_T, shift = rwkv7_branch(zr, shift0, s0, p)
    mixed = jax.nn.sigmoid(ga) * (att @ p['attn_up']) + jax.nn.sigmoid(gb) * (rw @ p['rwkv_up'])
    h = x + mixed @ p['w_out']
    out = h + peer_ffn(rmsnorm(h, p['norm2_g']), p)
    return out, k, v, ki, s_T, shift


def setup_inputs(seed: int = 0) -> dict:
    key = jax.random.key(seed)
    ks = jax.random.split(key, 32)
    nrm = lambda k, shape, s: jax.random.normal(k, shape, jnp.float32) * s
    L = DEPTH
    return {
        'x_prompt': nrm(ks[0], (BATCH, SEQ, D_MODEL), 1.0),
        'x_sample': nrm(ks[1], (DEC_BATCH, DEC_SEQ, D_MODEL), 1.0),
        'cache_k': nrm(ks[2], (L, DEC_BATCH, PAST_LEN, N_KV_HEADS, HEAD_DIM), 1.0),
        'cache_v': nrm(ks[3], (L, DEC_BATCH, PAST_LEN, N_KV_HEADS, HEAD_DIM), 1.0),
        'cache_idx_k': nrm(ks[4], (L, DEC_BATCH, PAST_LEN, IDX_DIM), 1.0),
        'state_wkv': nrm(ks[5], (L, DEC_BATCH, RWKV_HEADS, RWKV_HEAD, RWKV_HEAD), 0.3),
        'state_shift': nrm(ks[6], (L, DEC_BATCH, 1, RWKV_IN_W), 1.0),
        'norm1_g': 1.0 + nrm(ks[7], (L, D_MODEL), 0.02),
        'w_in': nrm(ks[8], (L, D_MODEL, IN_W), D_MODEL ** -0.5),
        'rwkv_mu': jax.random.uniform(ks[9], (L, RWKV_IN_W), jnp.float32),
        'rwkv_w0': jax.random.uniform(ks[10], (L, RWKV_W), jnp.float32, -4.0, 1.0),
        'rwkv_w2': nrm(ks[11], (L, DECAY_LORA, RWKV_W), DECAY_LORA ** -0.5),
        'rwkv_a0': nrm(ks[12], (L, RWKV_W), 0.3),
        'rwkv_a2': nrm(ks[13], (L, AAA_LORA, RWKV_W), 0.3 * AAA_LORA ** -0.5),
        'rwkv_g2': nrm(ks[14], (L, GATE_LORA, RWKV_W), GATE_LORA ** -0.5),
        'rwkv_k_k': 1.0 + nrm(ks[15], (L, RWKV_W), 0.1),
        'rwkv_k_a': 1.0 + nrm(ks[16], (L, RWKV_W), 0.02),
        'rwkv_r_k': nrm(ks[17], (L, RWKV_HEADS, RWKV_HEAD), 0.1),
        'rwkv_ln_w': 1.0 + nrm(ks[18], (L, RWKV_W), 0.02),
        'rwkv_ln_b': nrm(ks[19], (L, RWKV_W), 0.01),
        'attn_up': nrm(ks[20], (L, ATT_W, D_MODEL), ATT_W ** -0.5),
        'rwkv_up': nrm(ks[21], (L, RWKV_W, D_MODEL), RWKV_W ** -0.5),
        'w_out': nrm(ks[22], (L, D_MODEL, D_MODEL), D_MODEL ** -0.5),
        'norm2_g': 1.0 + nrm(ks[23], (L, D_MODEL), 0.02),
        'peer_wq': nrm(ks[24], (L, D_MODEL, PEER_HEADS * PEER_DK), D_MODEL ** -0.5),
        'peer_subkeys': nrm(ks[25], (L, 2, PEER_HEADS, PEER_KEYS, PEER_DK // 2), (PEER_DK // 2) ** -0.5),
        'peer_u': nrm(ks[26], (L, PEER_EXPERTS, D_MODEL), D_MODEL ** -0.5),
        'peer_v': nrm(ks[27], (L, PEER_EXPERTS, D_MODEL), 0.25),
        'final_g': 1.0 + nrm(ks[28], (D_MODEL,), 0.02),
    }


def reference(x_prompt, x_sample, cache_k, cache_v, cache_idx_k, state_wkv, state_shift,
              norm1_g, w_in, rwkv_mu, rwkv_w0, rwkv_w2, rwkv_a0, rwkv_a2, rwkv_g2,
              rwkv_k_k, rwkv_k_a, rwkv_r_k, rwkv_ln_w, rwkv_ln_b,
              attn_up, rwkv_up, w_out, norm2_g, peer_wq, peer_subkeys, peer_u, peer_v, final_g):
    bp, tp = x_prompt.shape[0], x_prompt.shape[1]
    ts = x_sample.shape[1]
    past = cache_k.shape[2]
    pos_p = jnp.arange(tp, dtype=jnp.int32)
    pos_s = past + jnp.arange(ts, dtype=jnp.int32)
    past_pos_s = jnp.arange(past, dtype=jnp.int32)
    dt = x_prompt.dtype
    empty_kv = jnp.zeros((bp, 0, N_KV_HEADS, HEAD_DIM), cache_k.dtype)
    empty_ki = jnp.zeros((bp, 0, IDX_DIM), cache_idx_k.dtype)
    empty_pos = jnp.zeros((0,), jnp.int32)
    s0_p = jnp.zeros((bp, RWKV_HEADS, RWKV_HEAD, RWKV_HEAD), state_wkv.dtype)
    sh0_p = jnp.zeros((bp, 1, RWKV_IN_W), dt)

    hp, hs = x_prompt, x_sample
    kp_l, vp_l, kip_l, sp_l, shp_l = [], [], [], [], []
    ks_l, vs_l, kis_l, ss_l, shs_l = [], [], [], [], []
    for l in range(DEPTH):
        p = {
            'norm1_g': norm1_g[l], 'w_in': w_in[l], 'rwkv_mu': rwkv_mu[l], 'rwkv_w0': rwkv_w0[l],
            'rwkv_w2': rwkv_w2[l], 'rwkv_a0': rwkv_a0[l], 'rwkv_a2': rwkv_a2[l], 'rwkv_g2': rwkv_g2[l],
            'rwkv_k_k': rwkv_k_k[l], 'rwkv_k_a': rwkv_k_a[l], 'rwkv_r_k': rwkv_r_k[l],
            'rwkv_ln_w': rwkv_ln_w[l], 'rwkv_ln_b': rwkv_ln_b[l], 'attn_up': attn_up[l],
            'rwkv_up': rwkv_up[l], 'w_out': w_out[l], 'norm2_g': norm2_g[l], 'peer_wq': peer_wq[l],
            'peer_subkeys': peer_subkeys[l], 'peer_u': peer_u[l], 'peer_v': peer_v[l],
        }
        hp, kp, vp, kip, sp, shp = trunk_layer(hp, pos_p, empty_kv, empty_kv, empty_ki, empty_pos, s0_p, sh0_p, p)
        hs, ksn, vsn, kisn, ssn, shsn = trunk_layer(hs, pos_s, cache_k[l], cache_v[l], cache_idx_k[l], past_pos_s,
                                                    state_wkv[l], state_shift[l], p)
        kp_l.append(kp); vp_l.append(vp); kip_l.append(kip); sp_l.append(sp); shp_l.append(shp)
        ks_l.append(ksn); vs_l.append(vsn); kis_l.append(kisn); ss_l.append(ssn); shs_l.append(shsn)

    y_prompt = rmsnorm(hp, final_g)
    y_sample = rmsnorm(hs, final_g)
    return (y_prompt, y_sample,
            jnp.stack(kp_l), jnp.stack(vp_l), jnp.stack(kip_l), jnp.stack(sp_l), jnp.stack(shp_l),
            jnp.stack(ks_l), jnp.stack(vs_l), jnp.stack(kis_l), jnp.stack(ss_l), jnp.stack(shs_l))
```

```python
import functools

import numpy as np
import jax
import jax.numpy as jnp
from jax import lax
from jax.experimental import pallas as pl
from jax.experimental.pallas import tpu as pltpu

f32, bf16, i32 = jnp.float32, jnp.bfloat16, jnp.int32

CHUNK = 64
HEAD_DIM = 128
N_HEADS = 8
N_KV_HEADS = 4
ATT_W = N_HEADS * HEAD_DIM
KV_W = N_KV_HEADS * HEAD_DIM
IDX_HEADS = 16
IDX_DIM = 64
IDX_W = IDX_HEADS * IDX_DIM
IDX_TOPK = 256
ROPE_THETA = 10000.0
RWKV_HEAD = 64
RWKV_HEADS = 16
RWKV_W = RWKV_HEADS * RWKV_HEAD
DECAY_LORA = 96
AAA_LORA = 96
GATE_LORA = 256
RWKV_SIZES = (RWKV_W, RWKV_W, RWKV_W, DECAY_LORA, AAA_LORA, GATE_LORA)
RWKV_IN_W = sum(RWKV_SIZES)
PEER_HEADS = 8
PEER_KEYS = 128
PEER_DK = 256
PEER_TOPK = 16
NORM_EPS = 1e-6
GN_EPS = 64e-5

LANES = 128
SUBLANES = 8
VMEM_LIMIT = 56 * 1024 * 1024

QKV_W = ATT_W + 2 * KV_W + IDX_W + LANES
O_Q, O_K, O_V, O_QI, O_KI = 0, ATT_W, ATT_W + KV_W, ATT_W + 2 * KV_W, ATT_W + 2 * KV_W + IDX_W
O_WI = O_KI + IDX_DIM
RW_PAD_W = 3 * RWKV_W + 2 * LANES + GATE_LORA
O_ZW, O_ZA, O_ZG = 3 * RWKV_W, 3 * RWKV_W + LANES, 3 * RWKV_W + 2 * LANES

NN = (((1,), (0,)), ((), ()))
NT = (((1,), (1,)), ((), ()))
TN = (((0,), (0,)), ((), ()))


def _dot(a, b, dims=NN):
    return lax.dot_general(a, b, dims, preferred_element_type=f32)


def _split2(x):
    hi = x.astype(bf16)
    lo = (x - hi.astype(f32)).astype(bf16)
    return hi, lo


def _split3(x):
    hi = x.astype(bf16)
    r = x - hi.astype(f32)
    mid = r.astype(bf16)
    lo = (r - mid.astype(f32)).astype(bf16)
    return hi, mid, lo


def _mm3(a, b, dims=NN):
    ah, al = _split2(a)
    bh, bl = _split2(b)
    return _dot(ah, bh, dims) + (_dot(al, bh, dims) + _dot(ah, bl, dims))


def _mm_x01(x, m01):
    h, m, l = _split3(x)
    return _dot(h, m01) + (_dot(m, m01) + _dot(l, m01))


def _mm_01x(m01, x):
    h, m, l = _split3(x)
    return _dot(m01, h) + (_dot(m01, m) + _dot(m01, l))


def _block(n, target):
    best = None
    for d in range(SUBLANES, min(n, target) + 1, SUBLANES):
        if n % d == 0:
            best = d
    return best if best is not None else n


def _params(sem):
    return pltpu.CompilerParams(dimension_semantics=sem, vmem_limit_bytes=VMEM_LIMIT)


def _const_spec(shape):
    nd = len(shape)
    return pl.BlockSpec(shape, lambda *_: (0,) * nd, pipeline_mode=pl.Buffered(1))


def _rmsnorm_kernel(x_ref, g_ref, o_ref):
    x = x_ref[...]
    ms = jnp.mean(x * x, axis=-1, keepdims=True)
    o_ref[...] = (x * lax.rsqrt(ms + NORM_EPS) * g_ref[...]).astype(o_ref.dtype)


def _rmsnorm(x, g, out_dtype):
    n, d = x.shape
    tm = _block(n, 512)
    return pl.pallas_call(
        _rmsnorm_kernel,
        grid=(n // tm,),
        in_specs=[pl.BlockSpec((tm, d), lambda i: (i, 0)), pl.BlockSpec((1, d), lambda i: (0, 0))],
        out_specs=pl.BlockSpec((tm, d), lambda i: (i, 0)),
        out_shape=jax.ShapeDtypeStruct((n, d), out_dtype),
        compiler_params=_params(("parallel",)),
        name="rmsnorm",
    )(x, g.reshape(1, d))


def _matmul_kernel(a_ref, b_ref, o_ref):
    o_ref[...] = _dot(a_ref[...], b_ref[...])


def _lane_block(n, target):
    return max(t for t in range(LANES, min(n, target) + 1, LANES) if n % t == 0)


def _matmul(a, b, tm_target, tn_target):
    m, k = a.shape
    _, n = b.shape
    tm = _block(m, tm_target)
    tn = _lane_block(n, tn_target)
    return pl.pallas_call(
        _matmul_kernel,
        grid=(m // tm, n // tn),
        in_specs=[pl.BlockSpec((tm, k), lambda i, j: (i, 0)), pl.BlockSpec((k, tn), lambda i, j: (0, j))],
        out_specs=pl.BlockSpec((tm, tn), lambda i, j: (i, j)),
        out_shape=jax.ShapeDtypeStruct((m, n), f32),
        compiler_params=_params(("parallel", "arbitrary")),
        name="in_proj",
    )(a, b)


def _rope_kernel(z_ref, ca_ref, sa_ref, cb_ref, sb_ref,
                 q_ref, kf_ref, kb_ref, vf_ref, vb_ref, qi_ref, kif_ref, kilo_ref, kihi_ref, wi_ref):
    ca, sa, cb, sb = ca_ref[...], sa_ref[...], cb_ref[...], sb_ref[...]
    lane = lax.broadcasted_iota(i32, ca.shape, 1)
    first_half = (lane % IDX_DIM) < (IDX_DIM // 2)

    def rope_a(x):
        return x * ca + pltpu.roll(x, HEAD_DIM // 2, 1) * sa

    def rope_b(x):
        partner = jnp.where(first_half, pltpu.roll(x, LANES - IDX_DIM // 2, 1), pltpu.roll(x, IDX_DIM // 2, 1))
        return x * cb + partner * sb

    scale = HEAD_DIM ** -0.5
    for h in range(N_HEADS):
        sl = slice(O_Q + h * LANES, O_Q + (h + 1) * LANES)
        q_ref[:, h * LANES:(h + 1) * LANES] = (rope_a(z_ref[:, sl]) * scale).astype(q_ref.dtype)
    for h in range(N_KV_HEADS):
        sl = slice(O_K + h * LANES, O_K + (h + 1) * LANES)
        kr = rope_a(z_ref[:, sl])
        kf_ref[:, h * LANES:(h + 1) * LANES] = kr
        kb_ref[:, h * LANES:(h + 1) * LANES] = kr.astype(kb_ref.dtype)
    v = z_ref[:, O_V:O_V + KV_W]
    vf_ref[...] = v
    vb_ref[...] = v.astype(vb_ref.dtype)
    for p in range(IDX_W // LANES):
        sl = slice(O_QI + p * LANES, O_QI + (p + 1) * LANES)
        qi_ref[:, p * LANES:(p + 1) * LANES] = rope_b(z_ref[:, sl]).astype(qi_ref.dtype)
    kiw = z_ref[:, O_KI:O_KI + LANES]
    kir = rope_b(kiw)
    lo = lane < IDX_DIM
    kif_ref[...] = kir[:, :IDX_DIM]
    kilo_ref[...] = jnp.where(lo, kir, 0.0).astype(kilo_ref.dtype)
    kihi_ref[...] = jnp.where(lo, 0.0, pltpu.roll(kir, IDX_DIM, 1)).astype(kihi_ref.dtype)
    wi_ref[...] = kiw[:, IDX_DIM:IDX_DIM + IDX_HEADS] * (IDX_HEADS ** -0.5 * IDX_DIM ** -0.5)


def _rope(z, ca, sa, cb, sb):
    n = z.shape[0]
    tm = _block(n, 256)
    row = lambda w: pl.BlockSpec((tm, w), lambda i: (i, 0))
    outs = [(ATT_W, bf16), (KV_W, f32), (KV_W, bf16), (KV_W, f32), (KV_W, bf16), (IDX_W, bf16),
            (IDX_DIM, f32), (LANES, bf16), (LANES, bf16), (IDX_HEADS, f32)]
    return pl.pallas_call(
        _rope_kernel,
        grid=(n // tm,),
        in_specs=[row(QKV_W), row(LANES), row(LANES), row(LANES), row(LANES)],
        out_specs=[row(w) for w, _ in outs],
        out_shape=[jax.ShapeDtypeStruct((n, w), dt) for w, dt in outs],
        compiler_params=_params(("parallel",)),
        name="rope",
    )(z, ca, sa, cb, sb)


def _rope_tables(pos):
    def tab(d):
        inv = jnp.power(ROPE_THETA, -jnp.arange(0, d, 2, dtype=f32) / d)
        ang = pos.astype(f32)[:, None] * inv[None, :]
        return jnp.cos(ang), jnp.sin(ang)
    c, s = tab(HEAD_DIM)
    ca, sa = jnp.concatenate([c, c], 1), jnp.concatenate([-s, s], 1)
    c, s = tab(IDX_DIM)
    cb, sb = jnp.concatenate([c, c, c, c], 1), jnp.concatenate([-s, s, -s, s], 1)
    return ca, sa, cb, sb


DSA_KB = 256
DSA_QB = 128
INT_MIN = -2 ** 31


def _key_to_float(u):
    key = u ^ jnp.int32(INT_MIN)
    bits = key ^ ((key >> 31) & jnp.int32(0x7FFFFFFF))
    return pltpu.bitcast(bits, f32)


def _dsa_kernel(nkb_ref, q_ref, qi_ref, wi_ref, qpos_ref, k_ref, vt_ref, kilo_ref, kihi_ref,
                o_ref, s_ref, p_ref, *, l_real, topk, idx_bits):
    kb_sz = DSA_KB
    nkb = nkb_ref[pl.program_id(0)]
    qpos = qpos_ref[0]
    qchunk = qpos >> 6
    ninf = jnp.float32(-jnp.inf)
    iota_k = lax.broadcasted_iota(i32, (kb_sz, DSA_QB), 0)

    def idx_body(kb, c):
        off = pl.multiple_of(kb * kb_sz, kb_sz)
        klo = kilo_ref[0, pl.ds(off, kb_sz), :]
        khi = kihi_ref[0, pl.ds(off, kb_sz), :]
        acc = jnp.zeros((kb_sz, DSA_QB), f32)
        for p in range(IDX_HEADS // 2):
            x = qi_ref[0, :, p * LANES:(p + 1) * LANES]
            r0 = _dot(klo, x, NT)
            r1 = _dot(khi, x, NT)
            acc = acc + jnp.maximum(r0, 0.0) * wi_ref[0, 2 * p:2 * p + 1, :]
            acc = acc + jnp.maximum(r1, 0.0) * wi_ref[0, 2 * p + 1:2 * p + 2, :]
        kidx = off + iota_k
        vis = ((kidx >> 6) <= qchunk) & (kidx < l_real)
        s_ref[pl.ds(off, kb_sz), :] = jnp.where(vis, acc, ninf)
        return c

    lax.fori_loop(0, nkb, idx_body, 0)

    def count(pred):
        def body(kb, acc):
            off = pl.multiple_of(kb * kb_sz, kb_sz)
            m = pred(s_ref[pl.ds(off, kb_sz), :], off + iota_k)
            ones = jnp.where(m, 1, 0).astype(i32)
            return acc + jnp.sum(ones.reshape(kb_sz // SUBLANES, SUBLANES, DSA_QB), axis=0)
        acc = lax.fori_loop(0, nkb, body, jnp.zeros((SUBLANES, DSA_QB), i32))
        return jnp.sum(acc, axis=0, keepdims=True)

    nvis = jnp.minimum((qchunk + 1) * CHUNK, l_real)
    search = nvis > topk

    def bis(it, t_u):
        trial = t_u | (jnp.int32(1) << (31 - it))
        f = _key_to_float(trial)
        cnt = count(lambda s, kidx: s >= f)
        return jnp.where(cnt >= topk, trial, t_u)

    t_u = lax.fori_loop(0, 32, bis, jnp.zeros((1, DSA_QB), i32))
    thr = jnp.where(search, _key_to_float(t_u), ninf)

    cge = count(lambda s, kidx: s >= thr)
    excess = search & (cge > topk)
    big = jnp.int32(2 ** 30)

    def tie_path():
        need = topk - count(lambda s, kidx: s > thr)

        def jb(it, jl):
            trial = jl | (jnp.int32(1) << (idx_bits - 1 - it))
            g = count(lambda s, kidx: (s == thr) & (kidx < trial))
            return jnp.where(g < need, trial, jl)

        jl = lax.fori_loop(0, idx_bits, jb, jnp.zeros((1, DSA_QB), i32))
        return jnp.where(excess, jl, big)

    any_excess = jnp.max(jnp.where(excess, 1, 0).astype(i32)) > 0
    jlim = lax.cond(any_excess, tie_path, lambda: jnp.full((1, DSA_QB), big, i32))
    jlim = jnp.where(search, jlim, -1)

    for g in range(N_KV_HEADS):
        qg = q_ref[0, :, 2 * g * LANES:(2 * g + 2) * LANES]
        q2 = jnp.concatenate([qg[:, :LANES], qg[:, LANES:]], axis=0)

        def p1(kb, mx):
            off = pl.multiple_of(kb * kb_sz, kb_sz)
            kblk = k_ref[0, pl.ds(off, kb_sz), g * LANES:(g + 1) * LANES]
            lg = _dot(kblk, q2, NT)
            s = s_ref[pl.ds(off, kb_sz), :]
            m = (s > thr) | ((s == thr) & ((off + iota_k) <= jlim))
            lg = jnp.concatenate([jnp.where(m, lg[:, :DSA_QB], ninf), jnp.where(m, lg[:, DSA_QB:], ninf)], axis=1)
            p_ref[pl.ds(off, kb_sz), :] = lg
            return jnp.maximum(mx, jnp.max(lg, axis=0, keepdims=True))

        mx = lax.fori_loop(0, nkb, p1, jnp.full((1, 2 * DSA_QB), ninf, f32))

        def p2(kb, carry):
            lsum, acc = carry
            off = pl.multiple_of(kb * kb_sz, kb_sz)
            e = jnp.exp(p_ref[pl.ds(off, kb_sz), :] - mx)
            lsum = lsum + jnp.sum(e, axis=0, keepdims=True)
            vt = vt_ref[0, kb, g * LANES:(g + 1) * LANES, :]
            return lsum, acc + _dot(vt, e.astype(bf16))

        lsum, acc = lax.fori_loop(0, nkb, p2, (jnp.zeros((1, 2 * DSA_QB), f32),
                                                jnp.zeros((HEAD_DIM, 2 * DSA_QB), f32)))
        o = acc / lsum
        o_ref[0, :, 2 * g * LANES:(2 * g + 1) * LANES] = o[:, :DSA_QB].T.astype(o_ref.dtype)
        o_ref[0, :, (2 * g + 1) * LANES:(2 * g + 2) * LANES] = o[:, DSA_QB:].T.astype(o_ref.dtype)


def _dsa(q, qi, wi, qpos, k, v, kilo, kihi, l_real, topk):
    b, tq, _ = q.shape
    l = k.shape[1]
    lp = -(-l // DSA_KB) * DSA_KB
    padk = lambda a: jnp.pad(a, ((0, 0), (0, lp - l), (0, 0)))
    k, v, kilo, kihi = padk(k), padk(v), padk(kilo), padk(kihi)
    vt = jnp.transpose(v.reshape(b, lp // DSA_KB, DSA_KB, KV_W), (0, 1, 3, 2))
    nqb = tq // DSA_QB
    wit = jnp.transpose(wi.reshape(b, nqb, DSA_QB, IDX_HEADS), (0, 1, 3, 2)).reshape(b * nqb, IDX_HEADS, DSA_QB)
    qpos_b = qpos.reshape(nqb, 1, DSA_QB)
    nvis_max = jnp.minimum((jnp.max(qpos_b, axis=(1, 2)) // CHUNK + 1) * CHUNK, l_real)
    nkb = jnp.tile((nvis_max + DSA_KB - 1) // DSA_KB, b).astype(i32)

    bi = lambda n, *_: (n // nqb, n % nqb, 0)
    kern = functools.partial(_dsa_kernel, l_real=l_real, topk=topk, idx_bits=int(lp).bit_length())
    return pl.pallas_call(
        kern,
        grid_spec=pltpu.PrefetchScalarGridSpec(
            num_scalar_prefetch=1,
            grid=(b * nqb,),
            in_specs=[
                pl.BlockSpec((1, DSA_QB, ATT_W), bi),
                pl.BlockSpec((1, DSA_QB, IDX_W), bi),
                pl.BlockSpec((1, IDX_HEADS, DSA_QB), lambda n, *_: (n, 0, 0)),
                pl.BlockSpec((1, 1, DSA_QB), lambda n, *_: (n % nqb, 0, 0)),
                pl.BlockSpec((1, lp, KV_W), lambda n, *_: (n // nqb, 0, 0)),
                pl.BlockSpec((1, lp // DSA_KB, KV_W, DSA_KB), lambda n, *_: (n // nqb, 0, 0, 0)),
                pl.BlockSpec((1, lp, LANES), lambda n, *_: (n // nqb, 0, 0)),
                pl.BlockSpec((1, lp, LANES), lambda n, *_: (n // nqb, 0, 0)),
            ],
            out_specs=pl.BlockSpec((1, DSA_QB, ATT_W), bi),
            scratch_shapes=[pltpu.VMEM((lp, DSA_QB), f32), pltpu.VMEM((lp, 2 * DSA_QB), f32)],
        ),
        out_shape=jax.ShapeDtypeStruct((b, tq, ATT_W), bf16),
        compiler_params=_params(("arbitrary",)),
        name="dsa",
    )(nkb, q, qi, wit, qpos_b, k, vt, kilo, kihi)


def _rwkv_prep_kernel(z_ref, pv_ref, mu_ref, w0_ref, w2_ref, a0_ref, a2_ref, g2_ref, kk_ref, ka_ref, rk_ref, bd_ref,
                      r_ref, lw_ref, k_ref, v_ref, a_ref, b_ref, g_ref, bonus_ref):
    z = z_ref[...]
    zm = z + (pv_ref[...] - z) * mu_ref[...]
    r = zm[:, 0:RWKV_W]
    k = zm[:, RWKV_W:2 * RWKV_W]
    v = zm[:, 2 * RWKV_W:3 * RWKV_W]
    zw = zm[:, O_ZW:O_ZW + LANES]
    za = zm[:, O_ZA:O_ZA + LANES]
    zg = zm[:, O_ZG:O_ZG + GATE_LORA]
    bd = bd_ref[...]
    x = -(w0_ref[...] + _mm3(jnp.tanh(zw), w2_ref[...]))
    softplus = jnp.maximum(x, 0.0) + jnp.log1p(jnp.exp(-jnp.abs(x)))
    lw_ref[...] = -jnp.exp(-softplus - 0.5)
    a_sig = jax.nn.sigmoid(a0_ref[...] + _mm3(za, a2_ref[...]))
    g_ref[...] = _mm3(jax.nn.sigmoid(zg), g2_ref[...])
    kk = k * kk_ref[...]
    ss = _mm_x01(kk * kk, bd)
    kk = kk / jnp.maximum(jnp.sqrt(ss), 1e-12)
    k2 = k * (1.0 + (a_sig - 1.0) * ka_ref[...])
    r_ref[...] = r
    k_ref[...] = k2
    v_ref[...] = v
    a_ref[...] = -kk
    b_ref[...] = kk * a_sig
    bonus_ref[...] = _mm_x01(r * k2 * rk_ref[...], bd) * v


def _head_blockdiag():
    h = jnp.arange(RWKV_W, dtype=i32) // RWKV_HEAD
    return (h[:, None] == h[None, :]).astype(bf16)


def _rwkv_prep(zr, prev, mu, w0, w2, a0, a2, g2, k_k, k_a, r_k, bd):
    n = zr.shape[0]
    tm = _block(n, 256)
    row = lambda w: pl.BlockSpec((tm, w), lambda i: (i, 0))
    vec = lambda w: _const_spec((1, w))
    padrows = lambda m: jnp.pad(m, ((0, LANES - m.shape[0]), (0, 0)))
    mu_p = jnp.concatenate([mu[:3 * RWKV_W], mu[3 * RWKV_W:3 * RWKV_W + DECAY_LORA], jnp.zeros((LANES - DECAY_LORA,), f32),
                            mu[3 * RWKV_W + DECAY_LORA:3 * RWKV_W + DECAY_LORA + AAA_LORA], jnp.zeros((LANES - AAA_LORA,), f32),
                            mu[3 * RWKV_W + DECAY_LORA + AAA_LORA:]])
    return pl.pallas_call(
        _rwkv_prep_kernel,
        grid=(n // tm,),
        in_specs=[row(RW_PAD_W), row(RW_PAD_W), vec(RW_PAD_W), vec(RWKV_W), _const_spec((LANES, RWKV_W)), vec(RWKV_W),
                  _const_spec((LANES, RWKV_W)), _const_spec((GATE_LORA, RWKV_W)), vec(RWKV_W), vec(RWKV_W), vec(RWKV_W),
                  _const_spec((RWKV_W, RWKV_W))],
        out_specs=[row(RWKV_W)] * 8,
        out_shape=[jax.ShapeDtypeStruct((n, RWKV_W), f32)] * 8,
        compiler_params=_params(("parallel",)),
        name="rwkv_prep",
    )(zr, prev, mu_p.reshape(1, -1), w0.reshape(1, -1), padrows(w2), a0.reshape(1, -1), padrows(a2), g2,
      k_k.reshape(1, -1), k_a.reshape(1, -1), r_k.reshape(1, -1), bd)


WKV_C = 64


def _wkv_chunk_terms(r, lw, k, v, a, b, tri):
    c = WKV_C
    cum = _mm_01x(tri, lw)
    p = jnp.exp(cum)
    pinv = jnp.exp(-cum)
    at = a * jnp.exp(cum - lw)
    bt = b * pinv
    kt = k * pinv
    rt = r * p
    g = _mm3(jnp.concatenate([at, rt], axis=0), jnp.concatenate([bt, kt], axis=0), NT)
    row = lax.broadcasted_iota(i32, (c, c), 0)
    col = lax.broadcasted_iota(i32, (c, c), 1)
    strict, incl = col < row, col <= row
    lab = jnp.where(strict, g[:c, :c], 0.0)
    lak = jnp.where(strict, g[:c, c:], 0.0)
    mrb = jnp.where(incl, g[c:, :c], 0.0)
    mrk = jnp.where(incl, g[c:, c:], 0.0)
    eye = jnp.where(row == col, 1.0, 0.0).astype(f32)
    x = eye + lab
    npow = lab
    for _ in range(int(np.log2(c)) - 1):
        npow = _mm3(npow, npow)
        x = x + _mm3(x, npow)
    q1 = _mm3(x, at)
    q2 = _mm3(x, _mm3(lak, v))
    m1 = eye + _mm3(q1, bt, TN)
    n2 = _mm3(q2, bt, TN) + _mm3(v, kt, TN)
    y1 = rt + _mm3(mrb, q1)
    y2 = _mm3(mrb, q2) + _mm3(mrk, v)
    return m1, n2, y1, y2, p[c - 1:c, :]


def _wkv_kernel(r_ref, lw_ref, k_ref, v_ref, a_ref, b_ref, s0_ref, tri_ref, y_ref, st_ref, s_scr, *, hb, nc):
    j = pl.program_id(1)

    @pl.when(j == 0)
    def _():
        s_scr[...] = s0_ref[...]

    tri = tri_ref[...]
    for h in range(hb):
        terms = []
        for ci in range(nc):
            sl = slice(ci * WKV_C, (ci + 1) * WKV_C)
            terms.append(_wkv_chunk_terms(r_ref[h, sl, :], lw_ref[h, sl, :], k_ref[h, sl, :], v_ref[h, sl, :],
                                          a_ref[h, sl, :], b_ref[h, sl, :], tri))
        s = s_scr[h]
        for ci in range(nc):
            m1, n2, y1, y2, pc = terms[ci]
            y_ref[h, ci * WKV_C:(ci + 1) * WKV_C, :] = _mm3(y1, s, NT) + y2
            s = (_mm3(s, m1) + n2) * pc
        s_scr[h] = s

    @pl.when(j == pl.num_programs(1) - 1)
    def _():
        st_ref[...] = s_scr[...]


def _wkv(r, lw, k, v, a, b, s0, hb, nc):
    bh, t, n = r.shape
    tb = WKV_C * nc
    tri = (jnp.arange(WKV_C)[:, None] >= jnp.arange(WKV_C)[None, :]).astype(bf16)
    seq = pl.BlockSpec((hb, tb, n), lambda i, j: (i, j, 0))
    st = pl.BlockSpec((hb, n, n), lambda i, j: (i, 0, 0))
    return pl.pallas_call(
        functools.partial(_wkv_kernel, hb=hb, nc=nc),
        grid=(bh // hb, t // tb),
        in_specs=[seq] * 6 + [st, _const_spec((WKV_C, WKV_C))],
        out_specs=[seq, st],
        out_shape=[jax.ShapeDtypeStruct((bh, t, n), f32), jax.ShapeDtypeStruct((bh, n, n), f32)],
        scratch_shapes=[pltpu.VMEM((hb, n, n), f32)],
        compiler_params=_params(("parallel", "arbitrary")),
        name="wkv",
    )(r, lw, k, v, a, b, s0, tri)


def _mix_kernel(att_ref, y_ref, bonus_ref, g_ref, ga_ref, gb_ref, x_ref, lnw_ref, lnb_ref, bd_ref,
                au_ref, ru_ref, wo_ref, h_ref):
    bd = bd_ref[...]
    y = y_ref[...]
    mean = _mm_x01(y, bd) * (1.0 / RWKV_HEAD)
    d = y - mean
    var = _mm_x01(d * d, bd) * (1.0 / RWKV_HEAD)
    yn = d * lax.rsqrt(var + GN_EPS) * lnw_ref[...] + lnb_ref[...]
    rw = ((yn + bonus_ref[...]) * g_ref[...]).astype(bf16)
    mixed = (jax.nn.sigmoid(ga_ref[...]) * _dot(att_ref[...], au_ref[...])
             + jax.nn.sigmoid(gb_ref[...]) * _dot(rw, ru_ref[...]))
    h_ref[...] = x_ref[...] + _dot(mixed.astype(bf16), wo_ref[...])


def _mix(att, y, bonus, g, gates, x, ln_w, ln_b, bd, attn_up, rwkv_up, w_out):
    n, d = x.shape
    tm = _block(n, 256)
    row = lambda w: pl.BlockSpec((tm, w), lambda i: (i, 0))
    return pl.pallas_call(
        _mix_kernel,
        grid=(n // tm,),
        in_specs=[row(ATT_W), row(RWKV_W), row(RWKV_W), row(RWKV_W),
                  pl.BlockSpec((tm, d), lambda i: (i, 0)), pl.BlockSpec((tm, d), lambda i: (i, 1)), row(d),
                  _const_spec((1, RWKV_W)), _const_spec((1, RWKV_W)), _const_spec((RWKV_W, RWKV_W)),
                  _const_spec((ATT_W, d)), _const_spec((RWKV_W, d)), _const_spec((d, d))],
        out_specs=row(d),
        out_shape=jax.ShapeDtypeStruct((n, d), f32),
        compiler_params=_params(("parallel",)),
        name="mix",
    )(att, y, bonus, g, gates, gates, x, ln_w.reshape(1, -1), ln_b.reshape(1, -1), bd, attn_up, rwkv_up, w_out)


def _top_sorted(s, n):
    vals = []
    for _ in range(n):
        m = jnp.max(s, axis=0, keepdims=True)
        vals.append(m)
        s = jnp.where(s == m, -jnp.inf, s)
    return vals


_PAIR_CANDS = [(a, b) for a in range(PEER_TOPK) for b in range(PEER_TOPK) if (a + 1) * (b + 1) <= PEER_TOPK]


def _peer_router_kernel(h_ref, g_ref, wq_ref, sk_ref, xt_ref, s1_ref, c1_ref, s2_ref, e2_ref, tau_ref):
    h = h_ref[...]
    ms = jnp.mean(h * h, axis=-1, keepdims=True)
    hn = h * lax.rsqrt(ms + NORM_EPS) * g_ref[...]
    xt_ref[...] = hn.T.astype(xt_ref.dtype)
    qp = _dot(hn.astype(bf16), wq_ref[...])
    half = PEER_DK // 2
    for hd in range(PEER_HEADS):
        q1 = qp[:, hd * PEER_DK:hd * PEER_DK + half].astype(bf16)
        q2 = qp[:, hd * PEER_DK + half:(hd + 1) * PEER_DK].astype(bf16)
        s1 = _dot(sk_ref[0, hd], q1, NT)
        s2 = _dot(sk_ref[1, hd], q2, NT)
        v1 = _top_sorted(s1, PEER_TOPK)
        v2 = _top_sorted(s2, PEER_TOPK)
        cands = [v1[a] + v2[b] for a, b in _PAIR_CANDS]
        pad = -(-len(cands) // SUBLANES) * SUBLANES - len(cands)
        cmat = jnp.concatenate(cands + [jnp.full_like(cands[0], -jnp.inf)] * pad, axis=0)
        sv = _top_sorted(cmat, PEER_TOPK)
        tau = sv[PEER_TOPK - 1]
        zsum = jnp.zeros_like(tau)
        for (a, b), c in zip(_PAIR_CANDS, cands):
            zsum = zsum + jnp.where(c >= tau, jnp.exp(v1[a] - v1[0]) * jnp.exp(v2[b] - v2[0]), 0.0)
        s1_ref[:, hd, :] = s1
        c1_ref[:, hd, :] = jnp.exp(s1 - v1[0]) / zsum
        s2_ref[hd] = s2
        e2_ref[hd] = jnp.exp(s2 - v2[0])
        tau_ref[hd:hd + 1, :] = tau


def _peer_router(h, g, wq, subkeys):
    n, d = h.shape
    tm = _lane_block(n, 256)
    hk = (PEER_KEYS, PEER_HEADS, n)
    kh = (PEER_HEADS, PEER_KEYS, n)
    return pl.pallas_call(
        _peer_router_kernel,
        grid=(n // tm,),
        in_specs=[pl.BlockSpec((tm, d), lambda i: (i, 0)), _const_spec((1, d)), _const_spec(wq.shape),
                  _const_spec(subkeys.shape)],
        out_specs=[pl.BlockSpec((d, tm), lambda i: (0, i)),
                   pl.BlockSpec((PEER_KEYS, PEER_HEADS, tm), lambda i: (0, 0, i)),
                   pl.BlockSpec((PEER_KEYS, PEER_HEADS, tm), lambda i: (0, 0, i)),
                   pl.BlockSpec((PEER_HEADS, PEER_KEYS, tm), lambda i: (0, 0, i)),
                   pl.BlockSpec((PEER_HEADS, PEER_KEYS, tm), lambda i: (0, 0, i)),
                   pl.BlockSpec((PEER_HEADS, tm), lambda i: (0, i))],
        out_shape=[jax.ShapeDtypeStruct((d, n), bf16), jax.ShapeDtypeStruct(hk, f32), jax.ShapeDtypeStruct(hk, f32),
                   jax.ShapeDtypeStruct(kh, f32), jax.ShapeDtypeStruct(kh, f32),
                   jax.ShapeDtypeStruct((PEER_HEADS, n), f32)],
        compiler_params=_params(("parallel",)),
        name="peer_router",
    )(h, g.reshape(1, d), wq, subkeys)


PEER_ET = 512
PEER_TT = 768


def _peer_dense(xt, u, vt, s1, c1, s2, e2, tau):
    d, n = xt.shape
    e = u.shape[0]
    tt = _lane_block(n, PEER_TT)
    n1 = PEER_ET // PEER_KEYS
    return pl.pallas_call(
        functools.partial(_peer_dense_kernel, tt=tt),
        grid=(n // tt, e // PEER_ET),
        in_specs=[pl.BlockSpec((d, tt), lambda i, j: (0, i)),
                  pl.BlockSpec((PEER_ET, d), lambda i, j: (j, 0)),
                  pl.BlockSpec((d, PEER_ET), lambda i, j: (0, j)),
                  pl.BlockSpec((n1, PEER_HEADS, tt), lambda i, j: (j, 0, i)),
                  pl.BlockSpec((n1, PEER_HEADS, tt), lambda i, j: (j, 0, i)),
                  pl.BlockSpec((PEER_HEADS, PEER_KEYS, tt), lambda i, j: (0, 0, i)),
                  pl.BlockSpec((PEER_HEADS, PEER_KEYS, tt), lambda i, j: (0, 0, i)),
                  pl.BlockSpec((PEER_HEADS, tt), lambda i, j: (0, i))],
        out_specs=pl.BlockSpec((d, tt), lambda i, j: (0, i)),
        out_shape=jax.ShapeDtypeStruct((d, n), f32),
        scratch_shapes=[pltpu.VMEM((PEER_ET, tt), f32), pltpu.VMEM((PEER_ET, tt), bf16)],
        compiler_params=_params(("parallel", "arbitrary")),
        name="peer_dense",
    )(xt, u, vt, s1, c1, s2, e2, tau)


def _peer_dense_kernel(xt_ref, u_ref, vt_ref, s1_ref, c1_ref, s2_ref, e2_ref, tau_ref, o_ref, pre_ref, gw_ref, *, tt):
    j = pl.program_id(1)

    @pl.when(j == 0)
    def _():
        o_ref[...] = jnp.zeros_like(o_ref)

    pre_ref[...] = _dot(u_ref[...], xt_ref[...])
    n1 = PEER_ET // PEER_KEYS
    for tc in range(tt // LANES):
        ls = slice(tc * LANES, (tc + 1) * LANES)
        for a in range(n1):
            acc = jnp.zeros((PEER_KEYS, LANES), f32)
            for hd in range(PEER_HEADS):
                c = s2_ref[hd, :, ls] + s1_ref[a, hd:hd + 1, ls]
                sel = c >= tau_ref[hd:hd + 1, ls]
                acc = acc + jnp.where(sel, e2_ref[hd, :, ls], 0.0) * c1_ref[a, hd:hd + 1, ls]
            pa = pre_ref[a * PEER_KEYS:(a + 1) * PEER_KEYS, ls]
            act = 0.5 * pa * (1.0 + lax.erf(pa * (2.0 ** -0.5)))
            gw_ref[a * PEER_KEYS:(a + 1) * PEER_KEYS, ls] = (acc * act).astype(gw_ref.dtype)
    o_ref[...] += _dot(vt_ref[...], gw_ref[...])


def _final_kernel(h_ref, pt_ref, g_ref, o_ref):
    x = h_ref[...] + pt_ref[...].T
    ms = jnp.mean(x * x, axis=-1, keepdims=True)
    o_ref[...] = x * lax.rsqrt(ms + NORM_EPS) * g_ref[...]


def _final(h, pt, g):
    n, d = h.shape
    tm = LANES
    return pl.pallas_call(
        _final_kernel,
        grid=(n // tm,),
        in_specs=[pl.BlockSpec((tm, d), lambda i: (i, 0)), pl.BlockSpec((d, tm), lambda i: (0, i)), _const_spec((1, d))],
        out_specs=pl.BlockSpec((tm, d), lambda i: (i, 0)),
        out_shape=jax.ShapeDtypeStruct((n, d), f32),
        compiler_params=_params(("parallel",)),
        name="final_norm",
    )(h, pt, g.reshape(1, d))


def _split_w_in(w_in, d):
    in_sizes = (ATT_W, KV_W, KV_W, IDX_W, IDX_DIM, IDX_HEADS, d, d, RWKV_IN_W)
    q, k, v, qi, ki, wi, ga, gb, zr = jnp.split(w_in, [int(c) for c in np.cumsum(in_sizes)[:-1]], axis=1)
    r_, k_, v_, zw, za, zg = jnp.split(zr, [int(c) for c in np.cumsum(RWKV_SIZES)[:-1]], axis=1)
    z = lambda n: jnp.zeros((d, n), w_in.dtype)
    w_qkv = jnp.concatenate([q, k, v, qi, ki, wi, z(LANES - IDX_DIM - IDX_HEADS)], axis=1)
    w_gate = jnp.concatenate([ga, gb], axis=1)
    w_rw = jnp.concatenate([r_, k_, v_, zw, z(LANES - DECAY_LORA), za, z(LANES - AAA_LORA), zg], axis=1)
    return w_qkv.astype(bf16), w_gate.astype(bf16), w_rw.astype(bf16)


def _unpad_rw(zr_pad):
    return jnp.concatenate([zr_pad[..., :3 * RWKV_W], zr_pad[..., O_ZW:O_ZW + DECAY_LORA],
                            zr_pad[..., O_ZA:O_ZA + AAA_LORA], zr_pad[..., O_ZG:]], axis=-1)


def _pad_rw(zr):
    cuts = [int(c) for c in np.cumsum(RWKV_SIZES)[:-1]]
    r_, k_, v_, zw, za, zg = jnp.split(zr, cuts, axis=-1)
    z = lambda n: jnp.zeros(zr.shape[:-1] + (n,), zr.dtype)
    return jnp.concatenate([r_, k_, v_, zw, z(LANES - DECAY_LORA), za, z(LANES - AAA_LORA), zg], axis=-1)


def _heads_first(x, b, t):
    return jnp.transpose(x.reshape(b, t, RWKV_HEADS, RWKV_HEAD), (0, 2, 1, 3)).reshape(b * RWKV_HEADS, t, RWKV_HEAD)


def _tokens_first(y, b, t):
    return jnp.transpose(y.reshape(b, RWKV_HEADS, t, RWKV_HEAD), (0, 2, 1, 3)).reshape(b * t, RWKV_W)


def kernel(x_prompt, x_sample, cache_k, cache_v, cache_idx_k, state_wkv, state_shift, norm1_g, w_in, rwkv_mu, rwkv_w0,
           rwkv_w2, rwkv_a0, rwkv_a2, rwkv_g2, rwkv_k_k, rwkv_k_a, rwkv_r_k, rwkv_ln_w, rwkv_ln_b, attn_up, rwkv_up,
           w_out, norm2_g, peer_wq, peer_subkeys, peer_u, peer_v, final_g):
    bp, tp, d = x_prompt.shape
    bs, ts, _ = x_sample.shape
    past = cache_k.shape[2]
    n_p, n_s = bp * tp, bs * ts
    n_all = n_p + n_s
    l = 0

    x = jnp.concatenate([x_prompt.reshape(n_p, d), x_sample.reshape(n_s, d)], axis=0)
    pos_p = jnp.arange(tp, dtype=i32)
    pos_s = past + jnp.arange(ts, dtype=i32)
    pos = jnp.concatenate([jnp.tile(pos_p, bp), jnp.tile(pos_s, bs)])

    w_qkv, w_gate, w_rw = _split_w_in(w_in[l], d)
    xn = _rmsnorm(x, norm1_g[l], bf16)
    z_qkv = _matmul(xn, w_qkv, 1024, 640)
    gates = _matmul(xn, w_gate, 1024, 1024)
    zr = _matmul(xn, w_rw, 1024, 896)

    ca, sa, cb, sb = _rope_tables(pos)
    q, k_f, k_b, v_f, v_b, qi, ki_f, kilo, kihi, wi = _rope(z_qkv, ca, sa, cb, sb)

    rp = lambda a: a[:n_p].reshape(bp, tp, a.shape[-1])
    rs = lambda a: a[n_p:].reshape(bs, ts, a.shape[-1])
    topk_p = min(IDX_TOPK, tp // 4)
    att_p = _dsa(rp(q), rp(qi), rp(wi), pos_p, rp(k_b), rp(v_b), rp(kilo), rp(kihi), tp, topk_p)

    ls = past + ts
    topk_s = min(IDX_TOPK, ls // 4)
    padq = lambda a: jnp.pad(a, ((0, 0), (0, DSA_QB - ts), (0, 0)))
    cat = lambda c, new: jnp.concatenate([c, new], axis=1)
    ck = cache_k[l].reshape(bs, past, KV_W).astype(bf16)
    cv = cache_v[l].reshape(bs, past, KV_W).astype(bf16)
    cki = cache_idx_k[l]
    zeros_ki = jnp.zeros_like(cki)
    cki_lo = jnp.concatenate([cki, zeros_ki], axis=-1).astype(bf16)
    cki_hi = jnp.concatenate([zeros_ki, cki], axis=-1).astype(bf16)
    qpos_s = jnp.concatenate([pos_s, jnp.full((DSA_QB - ts,), past, i32)])
    att_s = _dsa(padq(rs(q)), padq(rs(qi)), padq(rs(wi)), qpos_s, cat(ck, rs(k_b)), cat(cv, rs(v_b)),
                 cat(cki_lo, rs(kilo)), cat(cki_hi, rs(kihi)), ls, topk_s)[:, :ts]
    att = jnp.concatenate([att_p.reshape(n_p, ATT_W), att_s.reshape(n_s, ATT_W)], axis=0)

    zr_p, zr_s = rp(zr), rs(zr)
    shift0 = _pad_rw(state_shift[l])
    prev = jnp.concatenate([
        jnp.concatenate([jnp.zeros((bp, 1, RW_PAD_W), f32), zr_p[:, :-1]], axis=1).reshape(n_p, RW_PAD_W),
        jnp.concatenate([shift0, zr_s[:, :-1]], axis=1).reshape(n_s, RW_PAD_W)], axis=0)
    bd = _head_blockdiag()
    r_, lw_, k_, v_, a_, b_, g_, bonus = _rwkv_prep(zr, prev, rwkv_mu[l], rwkv_w0[l], rwkv_w2[l], rwkv_a0[l], rwkv_a2[l],
                                                    rwkv_g2[l], rwkv_k_k[l], rwkv_k_a[l], rwkv_r_k[l].reshape(-1), bd)
    seqs = (r_, lw_, k_, v_, a_, b_)
    nc_p = 4 if tp % (4 * WKV_C) == 0 else 1
    y_p, st_p = _wkv(*[_heads_first(a[:n_p], bp, tp) for a in seqs],
                     jnp.zeros((bp * RWKV_HEADS, RWKV_HEAD, RWKV_HEAD), f32), hb=2, nc=nc_p)
    ts_pad = -(-ts // WKV_C) * WKV_C
    pad_t = lambda a: jnp.pad(a, ((0, 0), (0, ts_pad - ts), (0, 0)))
    y_s, st_s = _wkv(*[pad_t(_heads_first(a[n_p:], bs, ts)) for a in seqs],
                     state_wkv[l].reshape(bs * RWKV_HEADS, RWKV_HEAD, RWKV_HEAD), hb=8, nc=ts_pad // WKV_C)
    y = jnp.concatenate([_tokens_first(y_p, bp, tp), _tokens_first(y_s[:, :ts], bs, ts)], axis=0)

    h = _mix(att, y, bonus, g_, gates, x, rwkv_ln_w[l], rwkv_ln_b[l], bd,
             attn_up[l].astype(bf16), rwkv_up[l].astype(bf16), w_out[l].astype(bf16))

    n_pad = -(-n_all // LANES) * LANES
    h_pad = jnp.pad(h, ((0, n_pad - n_all), (0, 0))) if n_pad != n_all else h
    xt, s1, c1, s2, e2, tau = _peer_router(h_pad, norm2_g[l], peer_wq[l].astype(bf16), peer_subkeys[l].astype(bf16))
    pt = _peer_dense(xt, peer_u[l].astype(bf16), jnp.transpose(peer_v[l]).astype(bf16), s1, c1, s2, e2, tau)
    y_all = _final(h_pad, pt, final_g)[:n_all]

    y_prompt = y_all[:n_p].reshape(bp, tp, d)
    y_sample = y_all[n_p:].reshape(bs, ts, d)
    kv = lambda a, b_, t: a.reshape(1, b_, t, N_KV_HEADS, HEAD_DIM)
    shift_p = _unpad_rw(zr_p[:, -1:])[None]
    shift_s = _unpad_rw(zr_s[:, -1:])[None]
    return (y_prompt, y_sample,
            kv(k_f[:n_p], bp, tp), kv(v_f[:n_p], bp, tp), ki_f[:n_p].reshape(1, bp, tp, IDX_DIM),
            st_p.reshape(1, bp, RWKV_HEADS, RWKV_HEAD, RWKV_HEAD), shift_p,
            kv(k_f[n_p:], bs, ts), kv(v_f[n_p:], bs, ts), ki_f[n_p:].reshape(1, bs, ts, IDX_DIM),
            st_s.reshape(1, bs, RWKV_HEADS, RWKV_HEAD, RWKV_HEAD), shift_s)
```

```python
import functools

import numpy as np
import jax
import jax.numpy as jnp
from jax import lax
from jax.experimental import pallas as pl
from jax.experimental.pallas import tpu as pltpu

f32, bf16, i32 = jnp.float32, jnp.bfloat16, jnp.int32

CHUNK = 64
HEAD_DIM = 128
N_HEADS = 8
N_KV_HEADS = 4
ATT_W = N_HEADS * HEAD_DIM
KV_W = N_KV_HEADS * HEAD_DIM
IDX_HEADS = 16
IDX_DIM = 64
IDX_W = IDX_HEADS * IDX_DIM
IDX_TOPK = 256
ROPE_THETA = 10000.0
RWKV_HEAD = 64
RWKV_HEADS = 16
RWKV_W = RWKV_HEADS * RWKV_HEAD
DECAY_LORA = 96
AAA_LORA = 96
GATE_LORA = 256
RWKV_SIZES = (RWKV_W, RWKV_W, RWKV_W, DECAY_LORA, AAA_LORA, GATE_LORA)
RWKV_IN_W = sum(RWKV_SIZES)
PEER_HEADS = 8
PEER_KEYS = 128
PEER_DK = 256
PEER_TOPK = 16
NORM_EPS = 1e-6
GN_EPS = 64e-5

LANES = 128
SUBLANES = 8
VMEM_LIMIT = 56 * 1024 * 1024

QKV_W = ATT_W + 2 * KV_W + IDX_W + LANES
O_Q, O_K, O_V, O_QI, O_KI = 0, ATT_W, ATT_W + KV_W, ATT_W + 2 * KV_W, ATT_W + 2 * KV_W + IDX_W
RW_PAD_W = 3 * RWKV_W + 2 * LANES + GATE_LORA
O_ZW, O_ZA, O_ZG = 3 * RWKV_W, 3 * RWKV_W + LANES, 3 * RWKV_W + 2 * LANES

NN = (((1,), (0,)), ((), ()))
NT = (((1,), (1,)), ((), ()))
TN = (((0,), (0,)), ((), ()))


def _dot(a, b, dims=NN):
    return lax.dot_general(a, b, dims, preferred_element_type=f32)


def _split2(x):
    hi = x.astype(bf16)
    lo = (x - hi.astype(f32)).astype(bf16)
    return hi, lo


def _split3(x):
    hi = x.astype(bf16)
    r = x - hi.astype(f32)
    mid = r.astype(bf16)
    lo = (r - mid.astype(f32)).astype(bf16)
    return hi, mid, lo


def _mm3(a, b, dims=NN):
    ah, al = _split2(a)
    bh, bl = _split2(b)
    return _dot(ah, bh, dims) + (_dot(al, bh, dims) + _dot(ah, bl, dims))


def _mmp(a, b, dims, passes):
    if passes == 1:
        return _dot(a.astype(bf16), b.astype(bf16), dims)
    return _mm3(a, b, dims)


def _mm_x01(x, m01):
    h, m, l = _split3(x)
    return _dot(h, m01) + (_dot(m, m01) + _dot(l, m01))


def _mm_01x(m01, x):
    h, m, l = _split3(x)
    return _dot(m01, h) + (_dot(m01, m) + _dot(m01, l))


def _block(n, target):
    best = None
    for d in range(SUBLANES, min(n, target) + 1, SUBLANES):
        if n % d == 0:
            best = d
    return best if best is not None else n


def _lane_block(n, target):
    return max(t for t in range(LANES, min(n, target) + 1, LANES) if n % t == 0)


def _params(sem):
    return pltpu.CompilerParams(dimension_semantics=sem, vmem_limit_bytes=VMEM_LIMIT)


def _const_spec(shape):
    nd = len(shape)
    return pl.BlockSpec(shape, lambda *_: (0,) * nd, pipeline_mode=pl.Buffered(1))


def _rmsnorm_kernel(x_ref, g_ref, o_ref):
    x = x_ref[...]
    ms = jnp.mean(x * x, axis=-1, keepdims=True)
    o_ref[...] = (x * lax.rsqrt(ms + NORM_EPS) * g_ref[...]).astype(o_ref.dtype)


def _rmsnorm(x, g, out_dtype):
    n, d = x.shape
    tm = _block(n, 512)
    return pl.pallas_call(
        _rmsnorm_kernel,
        grid=(n // tm,),
        in_specs=[pl.BlockSpec((tm, d), lambda i: (i, 0)), pl.BlockSpec((1, d), lambda i: (0, 0))],
        out_specs=pl.BlockSpec((tm, d), lambda i: (i, 0)),
        out_shape=jax.ShapeDtypeStruct((n, d), out_dtype),
        compiler_params=_params(("parallel",)),
        name="rmsnorm",
    )(x, g.reshape(1, d))


def _matmul_kernel(a_ref, b_ref, o_ref):
    o_ref[...] = _dot(a_ref[...], b_ref[...])


def _matmul(a, b, tm_target, tn_target):
    m, k = a.shape
    _, n = b.shape
    tm = _block(m, tm_target)
    tn = _lane_block(n, tn_target)
    return pl.pallas_call(
        _matmul_kernel,
        grid=(m // tm, n // tn),
        in_specs=[pl.BlockSpec((tm, k), lambda i, j: (i, 0)), pl.BlockSpec((k, tn), lambda i, j: (0, j))],
        out_specs=pl.BlockSpec((tm, tn), lambda i, j: (i, j)),
        out_shape=jax.ShapeDtypeStruct((m, n), f32),
        compiler_params=_params(("parallel", "arbitrary")),
        name="in_proj",
    )(a, b)


DSA_KB = 256
DSA_QB = 128


def _rope_kernel(z_ref, ca_ref, sa_ref, cb_ref, sb_ref,
                 q_ref, kf_ref, kb_ref, vf_ref, vt_ref, qi_ref, kif_ref, kilo_ref, kihi_ref, wi_ref):
    ca, sa, cb, sb = ca_ref[...], sa_ref[...], cb_ref[...], sb_ref[...]
    lane = lax.broadcasted_iota(i32, ca.shape, 1)
    first_half = (lane % IDX_DIM) < (IDX_DIM // 2)

    def rope_a(x):
        return x * ca + pltpu.roll(x, HEAD_DIM // 2, 1) * sa

    def rope_b(x):
        partner = jnp.where(first_half, pltpu.roll(x, LANES - IDX_DIM // 2, 1), pltpu.roll(x, IDX_DIM // 2, 1))
        return x * cb + partner * sb

    scale = HEAD_DIM ** -0.5
    for h in range(N_HEADS):
        sl = slice(O_Q + h * LANES, O_Q + (h + 1) * LANES)
        q_ref[:, h * LANES:(h + 1) * LANES] = (rope_a(z_ref[:, sl]) * scale).astype(q_ref.dtype)
    for h in range(N_KV_HEADS):
        sl = slice(O_K + h * LANES, O_K + (h + 1) * LANES)
        kr = rope_a(z_ref[:, sl])
        kf_ref[:, h * LANES:(h + 1) * LANES] = kr
        kb_ref[:, h * LANES:(h + 1) * LANES] = kr.astype(kb_ref.dtype)
        vh = z_ref[:, O_V + h * LANES:O_V + (h + 1) * LANES]
        vf_ref[:, h * LANES:(h + 1) * LANES] = vh
        vt_ref[0, h * LANES:(h + 1) * LANES, :] = vh.T.astype(vt_ref.dtype)
    for p in range(IDX_W // LANES):
        sl = slice(O_QI + p * LANES, O_QI + (p + 1) * LANES)
        qi_ref[:, p * LANES:(p + 1) * LANES] = rope_b(z_ref[:, sl]).astype(qi_ref.dtype)
    kiw = z_ref[:, O_KI:O_KI + LANES]
    kir = rope_b(kiw)
    lo = lane < IDX_DIM
    kif_ref[...] = kir[:, :IDX_DIM]
    kilo_ref[...] = jnp.where(lo, kir, 0.0).astype(kilo_ref.dtype)
    kihi_ref[...] = jnp.where(lo, 0.0, pltpu.roll(kir, IDX_DIM, 1)).astype(kihi_ref.dtype)
    wi_ref[...] = kiw[:, IDX_DIM:IDX_DIM + IDX_HEADS] * (IDX_HEADS ** -0.5 * IDX_DIM ** -0.5)


def _rope(z, ca, sa, cb, sb):
    n = z.shape[0]
    tm = DSA_KB
    assert n % tm == 0
    row = lambda w: pl.BlockSpec((tm, w), lambda i: (i, 0))
    outs = [(ATT_W, bf16), (KV_W, f32), (KV_W, bf16), (KV_W, f32), None, (IDX_W, bf16),
            (IDX_DIM, f32), (LANES, bf16), (LANES, bf16), (IDX_HEADS, f32)]
    vt_spec = pl.BlockSpec((1, KV_W, tm), lambda i: (i, 0, 0))
    vt_shape = jax.ShapeDtypeStruct((n // tm, KV_W, tm), bf16)
    return pl.pallas_call(
        _rope_kernel,
        grid=(n // tm,),
        in_specs=[row(QKV_W), row(LANES), row(LANES), row(LANES), row(LANES)],
        out_specs=[vt_spec if o is None else row(o[0]) for o in outs],
        out_shape=[vt_shape if o is None else jax.ShapeDtypeStruct((n, o[0]), o[1]) for o in outs],
        compiler_params=_params(("parallel",)),
        name="rope",
    )(z, ca, sa, cb, sb)


def _rope_tables(pos):
    def tab(d):
        inv = jnp.power(ROPE_THETA, -jnp.arange(0, d, 2, dtype=f32) / d)
        ang = pos.astype(f32)[:, None] * inv[None, :]
        return jnp.cos(ang), jnp.sin(ang)
    c, s = tab(HEAD_DIM)
    ca, sa = jnp.concatenate([c, c], 1), jnp.concatenate([-s, s], 1)
    c, s = tab(IDX_DIM)
    cb, sb = jnp.concatenate([c, c, c, c], 1), jnp.concatenate([-s, s, -s, s], 1)
    return ca, sa, cb, sb


INT_MIN = -2 ** 31


def _key_to_float(u):
    key = u ^ jnp.int32(INT_MIN)
    bits = key ^ ((key >> 31) & jnp.int32(0x7FFFFFFF))
    return pltpu.bitcast(bits, f32)


def _dsa_kernel(nkb_ref, q_ref, qi_ref, wi_ref, qpos_ref, k_ref, vt_ref, kilo_ref, kihi_ref,
                o_ref, s_ref, *, l_real, topk, idx_bits):
    kb_sz = DSA_KB
    nkb = nkb_ref[pl.program_id(0)]
    qpos = qpos_ref[0]
    qchunk = qpos >> 6
    ninf = jnp.float32(-jnp.inf)
    iota_k = lax.broadcasted_iota(i32, (kb_sz, DSA_QB), 0)

    def idx_body(kb, c):
        off = pl.multiple_of(kb * kb_sz, kb_sz)
        klo = kilo_ref[pl.ds(off, kb_sz), :]
        khi = kihi_ref[pl.ds(off, kb_sz), :]
        acc = jnp.zeros((kb_sz, DSA_QB), f32)
        for pp in range(IDX_HEADS // 4):
            x = jnp.concatenate([qi_ref[:, (2 * pp) * LANES:(2 * pp + 1) * LANES],
                                 qi_ref[:, (2 * pp + 1) * LANES:(2 * pp + 2) * LANES]], axis=0)
            r0 = _dot(klo, x, NT)
            r1 = _dot(khi, x, NT)
            h0 = 4 * pp
            acc = acc + jnp.maximum(r0[:, :DSA_QB], 0.0) * wi_ref[0, h0:h0 + 1, :]
            acc = acc + jnp.maximum(r1[:, :DSA_QB], 0.0) * wi_ref[0, h0 + 1:h0 + 2, :]
            acc = acc + jnp.maximum(r0[:, DSA_QB:], 0.0) * wi_ref[0, h0 + 2:h0 + 3, :]
            acc = acc + jnp.maximum(r1[:, DSA_QB:], 0.0) * wi_ref[0, h0 + 3:h0 + 4, :]
        kidx = off + iota_k
        vis = ((kidx >> 6) <= qchunk) & (kidx < l_real)
        s_ref[pl.ds(off, kb_sz), :] = jnp.where(vis, acc, ninf)
        return c

    lax.fori_loop(0, nkb, idx_body, 0)

    def count(pred):
        def body(kb, acc):
            off = pl.multiple_of(kb * kb_sz, kb_sz)
            m = pred(s_ref[pl.ds(off, kb_sz), :], off + iota_k)
            ones = jnp.where(m, 1, 0).astype(i32)
            return acc + jnp.sum(ones.reshape(kb_sz // SUBLANES, SUBLANES, DSA_QB), axis=0)
        acc = lax.fori_loop(0, nkb, body, jnp.zeros((SUBLANES, DSA_QB), i32))
        return jnp.sum(acc, axis=0, keepdims=True)

    nvis = jnp.minimum((qchunk + 1) * CHUNK, l_real)
    search = nvis > topk

    def bis(it, t_u):
        trial = t_u | (jnp.int32(1) << (31 - it))
        f = _key_to_float(trial)
        cnt = count(lambda s, kidx: s >= f)
        return jnp.where(cnt >= topk, trial, t_u)

    t_u = lax.fori_loop(0, 32, bis, jnp.zeros((1, DSA_QB), i32))
    thr = jnp.where(search, _key_to_float(t_u), ninf)

    cge = count(lambda s, kidx: s >= thr)
    excess = search & (cge > topk)
    big = jnp.int32(2 ** 30)

    def tie_path():
        need = topk - count(lambda s, kidx: s > thr)

        def jb(it, jl):
            trial = jl | (jnp.int32(1) << (idx_bits - 1 - it))
            g = count(lambda s, kidx: (s == thr) & (kidx < trial))
            return jnp.where(g < need, trial, jl)

        jl = lax.fori_loop(0, idx_bits, jb, jnp.zeros((1, DSA_QB), i32))
        return jnp.where(excess, jl, big)

    any_excess = jnp.max(jnp.where(excess, 1, 0).astype(i32)) > 0
    jlim = lax.cond(any_excess, tie_path, lambda: jnp.full((1, DSA_QB), big, i32))
    jlim = jnp.where(search, jlim, -1)

    q2 = []
    for g in range(N_KV_HEADS):
        q2.append(jnp.concatenate([q_ref[:, 2 * g * LANES:(2 * g + 1) * LANES],
                                   q_ref[:, (2 * g + 1) * LANES:(2 * g + 2) * LANES]], axis=0))

    def masked_logits(kb):
        off = pl.multiple_of(kb * kb_sz, kb_sz)
        s = s_ref[pl.ds(off, kb_sz), :]
        m = (s > thr) | ((s == thr) & ((off + iota_k) <= jlim))
        out = []
        for g in range(N_KV_HEADS):
            lg = _dot(k_ref[pl.ds(off, kb_sz), g * LANES:(g + 1) * LANES], q2[g], NT)
            out.append(jnp.concatenate([jnp.where(m, lg[:, :DSA_QB], ninf), jnp.where(m, lg[:, DSA_QB:], ninf)], axis=1))
        return out

    def p1(kb, mx):
        lgs = masked_logits(kb)
        return tuple(jnp.maximum(mx[g], jnp.max(lgs[g], axis=0, keepdims=True)) for g in range(N_KV_HEADS))

    mx = lax.fori_loop(0, nkb, p1, tuple(jnp.full((1, 2 * DSA_QB), ninf, f32) for _ in range(N_KV_HEADS)))

    def p2(kb, carry):
        lgs = masked_logits(kb)
        new = []
        for g in range(N_KV_HEADS):
            lsum, acc = carry[g]
            e = jnp.exp(lgs[g] - mx[g])
            vt = vt_ref[kb, g * LANES:(g + 1) * LANES, :]
            new.append((lsum + jnp.sum(e, axis=0, keepdims=True), acc + _dot(vt, e.astype(bf16))))
        return tuple(new)

    init = tuple((jnp.zeros((1, 2 * DSA_QB), f32), jnp.zeros((HEAD_DIM, 2 * DSA_QB), f32)) for _ in range(N_KV_HEADS))
    res = lax.fori_loop(0, nkb, p2, init)
    for g in range(N_KV_HEADS):
        lsum, acc = res[g]
        o = acc / lsum
        o_ref[:, 2 * g * LANES:(2 * g + 1) * LANES] = o[:, :DSA_QB].T.astype(o_ref.dtype)
        o_ref[:, (2 * g + 1) * LANES:(2 * g + 2) * LANES] = o[:, DSA_QB:].T.astype(o_ref.dtype)


def _dsa(q, qi, wit, qpos_b, k, vt, kilo, kihi, n_streams, lp, l_real, topk):
    nqb = qpos_b.shape[0]
    nblk = n_streams * nqb
    assert lp % DSA_KB == 0
    nvis_max = jnp.minimum((jnp.max(qpos_b, axis=(1, 2)) // CHUNK + 1) * CHUNK, l_real)
    nkb = jnp.tile((nvis_max + DSA_KB - 1) // DSA_KB, n_streams).astype(i32)
    kern = functools.partial(_dsa_kernel, l_real=l_real, topk=topk, idx_bits=int(lp).bit_length())
    return pl.pallas_call(
        kern,
        grid_spec=pltpu.PrefetchScalarGridSpec(
            num_scalar_prefetch=1,
            grid=(nblk,),
            in_specs=[
                pl.BlockSpec((DSA_QB, ATT_W), lambda n, *_: (n, 0)),
                pl.BlockSpec((DSA_QB, IDX_W), lambda n, *_: (n, 0)),
                pl.BlockSpec((1, IDX_HEADS, DSA_QB), lambda n, *_: (n, 0, 0)),
                pl.BlockSpec((1, 1, DSA_QB), lambda n, *_: (n % nqb, 0, 0)),
                pl.BlockSpec((lp, KV_W), lambda n, *_: (n // nqb, 0)),
                pl.BlockSpec((lp // DSA_KB, KV_W, DSA_KB), lambda n, *_: (n // nqb, 0, 0)),
                pl.BlockSpec((lp, LANES), lambda n, *_: (n // nqb, 0)),
                pl.BlockSpec((lp, LANES), lambda n, *_: (n // nqb, 0)),
            ],
            out_specs=pl.BlockSpec((DSA_QB, ATT_W), lambda n, *_: (n, 0)),
            scratch_shapes=[pltpu.VMEM((lp, DSA_QB), f32)],
        ),
        out_shape=jax.ShapeDtypeStruct((nblk * DSA_QB, ATT_W), bf16),
        compiler_params=_params(("arbitrary",)),
        name="dsa",
    )(nkb, q, qi, wit, qpos_b, k, vt, kilo, kihi)


def _rwkv_prep_kernel(z_ref, pv_ref, mu_ref, w0_ref, w2_ref, a0_ref, a2_ref, g2_ref, kk_ref, ka_ref, rk_ref, bd_ref,
                      r_ref, lw_ref, k_ref, v_ref, a_ref, b_ref, g_ref, bonus_ref, *, seq_len):
    z = z_ref[...]
    if seq_len is None:
        prev = pv_ref[...]
    else:
        tm = z.shape[0]
        at_start = (pl.program_id(0) * tm) % seq_len == 0
        boundary = jnp.where(at_start, 0.0, pv_ref[SUBLANES - 1:SUBLANES, :])
        first = lax.broadcasted_iota(i32, (tm, 1), 0) == 0
        prev = jnp.where(first, boundary, pltpu.roll(z, 1, 0))
    zm = z + (prev - z) * mu_ref[...]
    r = zm[:, 0:RWKV_W]
    k = zm[:, RWKV_W:2 * RWKV_W]
    v = zm[:, 2 * RWKV_W:3 * RWKV_W]
    zw = zm[:, O_ZW:O_ZW + LANES]
    za = zm[:, O_ZA:O_ZA + LANES]
    zg = zm[:, O_ZG:O_ZG + GATE_LORA]
    bd = bd_ref[...]
    x = -(w0_ref[...] + _mm3(jnp.tanh(zw), w2_ref[...]))
    softplus = jnp.maximum(x, 0.0) + jnp.log1p(jnp.exp(-jnp.abs(x)))
    lw_ref[...] = -jnp.exp(-softplus - 0.5)
    a_sig = jax.nn.sigmoid(a0_ref[...] + _mm3(za, a2_ref[...]))
    g_ref[...] = _mm3(jax.nn.sigmoid(zg), g2_ref[...])
    kk = k * kk_ref[...]
    ss = _mm_x01(kk * kk, bd)
    kk = kk / jnp.maximum(jnp.sqrt(ss), 1e-12)
    k2 = k * (1.0 + (a_sig - 1.0) * ka_ref[...])
    r_ref[...] = r
    k_ref[...] = k2
    v_ref[...] = v
    a_ref[...] = -kk
    b_ref[...] = kk * a_sig
    bonus_ref[...] = _mm_x01(r * k2 * rk_ref[...], bd) * v


def _head_blockdiag():
    h = jnp.arange(RWKV_W, dtype=i32) // RWKV_HEAD
    return (h[:, None] == h[None, :]).astype(bf16)


def _rwkv_prep(zr, prev, n_rows, seq_len, weights):
    mu_p, w0, w2p, a0, a2p, g2, k_k, k_a, r_k, bd = weights
    tm = _block(n_rows, 256)
    row = lambda w: pl.BlockSpec((tm, w), lambda i: (i, 0))
    vec = lambda w: _const_spec((1, w))
    if seq_len is None:
        pv, pv_spec = prev, row(RW_PAD_W)
    else:
        assert seq_len % tm == 0 and tm % SUBLANES == 0
        pv = zr
        pv_spec = pl.BlockSpec((SUBLANES, RW_PAD_W), lambda i: (jnp.maximum(i * (tm // SUBLANES) - 1, 0), 0))
    return pl.pallas_call(
        functools.partial(_rwkv_prep_kernel, seq_len=seq_len),
        grid=(n_rows // tm,),
        in_specs=[row(RW_PAD_W), pv_spec, vec(RW_PAD_W), vec(RWKV_W), _const_spec((LANES, RWKV_W)), vec(RWKV_W),
                  _const_spec((LANES, RWKV_W)), _const_spec((GATE_LORA, RWKV_W)), vec(RWKV_W), vec(RWKV_W), vec(RWKV_W),
                  _const_spec((RWKV_W, RWKV_W))],
        out_specs=[row(RWKV_W)] * 8,
        out_shape=[jax.ShapeDtypeStruct((n_rows, RWKV_W), f32)] * 8,
        compiler_params=_params(("parallel",)),
        name="rwkv_prep",
    )(zr, pv, mu_p, w0, w2p, a0, a2p, g2, k_k, k_a, r_k, bd)


def _rwkv_weights(mu, w0, w2, a0, a2, g2, k_k, k_a, r_k, bd):
    padrows = lambda m: jnp.pad(m, ((0, LANES - m.shape[0]), (0, 0)))
    c1, c2 = 3 * RWKV_W + DECAY_LORA, 3 * RWKV_W + DECAY_LORA + AAA_LORA
    mu_p = jnp.concatenate([mu[:3 * RWKV_W], mu[3 * RWKV_W:c1], jnp.zeros((LANES - DECAY_LORA,), f32),
                            mu[c1:c2], jnp.zeros((LANES - AAA_LORA,), f32), mu[c2:]])
    row = lambda a: a.reshape(1, -1)
    return (row(mu_p), row(w0), padrows(w2), row(a0), padrows(a2), g2, row(k_k), row(k_a), row(r_k), bd)


WKV_C = 64
WKV_PASSES = dict(gram=1, inv=1, apply=1, state=3, out=1)


def _wkv_kernel(r_ref, lw_ref, k_ref, v_ref, a_ref, b_ref, s0_ref, tri_ref, y_ref, st_ref, s_scr):
    j = pl.program_id(1)

    @pl.when(j == 0)
    def _():
        s_scr[...] = s0_ref[0]

    c, c2 = WKV_C, 2 * WKV_C
    ps = WKV_PASSES
    lw = lw_ref[0]
    cum = _mm_01x(tri_ref[...], lw)
    p = jnp.exp(cum)
    pinv = jnp.exp(-cum)
    at = a_ref[0] * jnp.exp(cum - lw)
    bt = b_ref[0] * pinv
    kt = k_ref[0] * pinv
    rt = r_ref[0] * p
    v = v_ref[0]
    lo = lax.broadcasted_iota(i32, (c, LANES), 1) < RWKV_HEAD
    row = lax.broadcasted_iota(i32, (c2, c2), 0)
    col = lax.broadcasted_iota(i32, (c2, c2), 1)
    strict = (col & (c - 1)) < (row & (c - 1))
    incl = (col & (c - 1)) <= (row & (c - 1))
    eye = jnp.where(row == col, 1.0, 0.0).astype(f32)
    pairs = range(lw.shape[1] // LANES)

    def stack(x, pr):
        xp = x[:, pr * LANES:(pr + 1) * LANES]
        return jnp.concatenate([jnp.where(lo, xp, 0.0), jnp.where(lo, 0.0, xp)], axis=0)

    at2 = [stack(at, pr) for pr in pairs]
    bt2 = [stack(bt, pr) for pr in pairs]
    kt2 = [stack(kt, pr) for pr in pairs]
    rt2 = [stack(rt, pr) for pr in pairs]
    v2 = [stack(v, pr) for pr in pairs]
    bk2 = [jnp.concatenate([bt2[pr], kt2[pr]], axis=0) for pr in pairs]
    g = [_mmp(jnp.concatenate([at2[pr], rt2[pr]], axis=0), bk2[pr], NT, ps["gram"]) for pr in pairs]
    lab = [jnp.where(strict, g[pr][:c2, :c2], 0.0) for pr in pairs]
    lak = [jnp.where(strict, g[pr][:c2, c2:], 0.0) for pr in pairs]
    mrb = [jnp.where(incl, g[pr][c2:, :c2], 0.0) for pr in pairs]
    mrk = [jnp.where(incl, g[pr][c2:, c2:], 0.0) for pr in pairs]
    x = [eye + lab[pr] for pr in pairs]
    npow = lab
    for _ in range(int(np.log2(c)) - 1):
        npow = [_mmp(npow[pr], npow[pr], NN, ps["inv"]) for pr in pairs]
        x = [x[pr] + _mmp(x[pr], npow[pr], NN, ps["inv"]) for pr in pairs]
    lv = [_mmp(lak[pr], v2[pr], NN, ps["apply"]) for pr in pairs]
    q = [_mmp(x[pr], jnp.concatenate([at2[pr], lv[pr]], axis=1), NN, ps["apply"]) for pr in pairs]
    m1 = [eye + _mmp(q[pr][:, :c2], bt2[pr], TN, ps["state"]) for pr in pairs]
    n2 = [_mmp(jnp.concatenate([q[pr][:, c2:], v2[pr]], axis=0), bk2[pr], TN, ps["state"]) for pr in pairs]
    yq = [_mmp(mrb[pr], q[pr], NN, ps["out"]) for pr in pairs]
    y1 = [rt2[pr] + yq[pr][:, :c2] for pr in pairs]
    y2 = [yq[pr][:, c2:] + _mmp(mrk[pr], v2[pr], NN, ps["out"]) for pr in pairs]
    for pr in pairs:
        s = s_scr[pr]
        yy = _mmp(y1[pr], s, NT, ps["out"]) + y2[pr]
        y_ref[0, :, pr * LANES:(pr + 1) * LANES] = yy[:c] + yy[c:]
        pc = p[c - 1:c, pr * LANES:(pr + 1) * LANES]
        s_scr[pr] = (_mmp(s, m1[pr], NN, ps["state"]) + n2[pr]) * pc

    @pl.when(j == pl.num_programs(1) - 1)
    def _():
        st_ref[0] = s_scr[...]


def _wkv(r, lw, k, v, a, b, s0):
    bsz, t, w = r.shape
    npairs = w // LANES
    tri = (jnp.arange(WKV_C)[:, None] >= jnp.arange(WKV_C)[None, :]).astype(bf16)
    eye2 = jnp.eye(2, dtype=f32)
    s0_bd = (s0.reshape(bsz, npairs, 2, RWKV_HEAD, 1, RWKV_HEAD) * eye2[None, None, :, None, :, None]
             ).reshape(bsz, npairs, LANES, LANES)
    seq = pl.BlockSpec((1, WKV_C, w), lambda i, j: (i, j, 0))
    st = pl.BlockSpec((1, npairs, LANES, LANES), lambda i, j: (i, 0, 0, 0))
    y, st_bd = pl.pallas_call(
        _wkv_kernel,
        grid=(bsz, t // WKV_C),
        in_specs=[seq] * 6 + [st, _const_spec((WKV_C, WKV_C))],
        out_specs=[seq, st],
        out_shape=[jax.ShapeDtypeStruct((bsz, t, w), f32), jax.ShapeDtypeStruct((bsz, npairs, LANES, LANES), f32)],
        scratch_shapes=[pltpu.VMEM((npairs, LANES, LANES), f32)],
        compiler_params=_params(("parallel", "arbitrary")),
        name="wkv",
    )(r, lw, k, v, a, b, s0_bd, tri)
    st6 = st_bd.reshape(bsz, npairs, 2, RWKV_HEAD, 2, RWKV_HEAD)
    s_t = jnp.stack([st6[:, :, 0, :, 0, :], st6[:, :, 1, :, 1, :]], axis=2).reshape(bsz, 2 * npairs, RWKV_HEAD, RWKV_HEAD)
    return y, s_t


def _mix_kernel(att_ref, y_ref, bonus_ref, g_ref, ga_ref, gb_ref, x_ref, lnw_ref, lnb_ref, bd_ref,
                au_ref, ru_ref, wo_ref, h_ref):
    bd = bd_ref[...]
    y = y_ref[...]
    mean = _mm_x01(y, bd) * (1.0 / RWKV_HEAD)
    d = y - mean
    var = _mm_x01(d * d, bd) * (1.0 / RWKV_HEAD)
    yn = d * lax.rsqrt(var + GN_EPS) * lnw_ref[...] + lnb_ref[...]
    rw = ((yn + bonus_ref[...]) * g_ref[...]).astype(bf16)
    mixed = (jax.nn.sigmoid(ga_ref[...]) * _dot(att_ref[...], au_ref[...])
             + jax.nn.sigmoid(gb_ref[...]) * _dot(rw, ru_ref[...]))
    h_ref[...] = x_ref[...] + _dot(mixed.astype(bf16), wo_ref[...])


def _mix(att, y, bonus, g, gates, gate_row0, x, ln_w, ln_b, bd, attn_up, rwkv_up, w_out):
    n, d = x.shape
    tm = _block(n, 256)
    assert gate_row0 % tm == 0
    goff = gate_row0 // tm
    row = lambda w: pl.BlockSpec((tm, w), lambda i: (i, 0))
    return pl.pallas_call(
        _mix_kernel,
        grid=(n // tm,),
        in_specs=[row(ATT_W), row(RWKV_W), row(RWKV_W), row(RWKV_W),
                  pl.BlockSpec((tm, d), lambda i: (i + goff, 0)), pl.BlockSpec((tm, d), lambda i: (i + goff, 1)), row(d),
                  _const_spec((1, RWKV_W)), _const_spec((1, RWKV_W)), _const_spec((RWKV_W, RWKV_W)),
                  _const_spec((ATT_W, d)), _const_spec((RWKV_W, d)), _const_spec((d, d))],
        out_specs=row(d),
        out_shape=jax.ShapeDtypeStruct((n, d), f32),
        compiler_params=_params(("parallel",)),
        name="mix",
    )(att, y, bonus, g, gates, gates, x, ln_w.reshape(1, -1), ln_b.reshape(1, -1), bd, attn_up, rwkv_up, w_out)


def _top_sorted(s, n):
    vals = []
    for _ in range(n):
        m = jnp.max(s, axis=0, keepdims=True)
        vals.append(m)
        s = jnp.where(s == m, -jnp.inf, s)
    return vals


_PAIR_CANDS = [(a, b) for a in range(PEER_TOPK) for b in range(PEER_TOPK) if (a + 1) * (b + 1) <= PEER_TOPK]


def _peer_router_kernel(h_ref, g_ref, wq_ref, sk_ref, xt_ref, s1_ref, c1_ref, s2_ref, e2_ref, tau_ref):
    h = h_ref[...]
    ms = jnp.mean(h * h, axis=-1, keepdims=True)
    hn = h * lax.rsqrt(ms + NORM_EPS) * g_ref[...]
    xt_ref[...] = hn.T.astype(xt_ref.dtype)
    qp = _dot(hn.astype(bf16), wq_ref[...])
    half = PEER_DK // 2
    for hd in range(PEER_HEADS):
        q1 = qp[:, hd * PEER_DK:hd * PEER_DK + half].astype(bf16)
        q2 = qp[:, hd * PEER_DK + half:(hd + 1) * PEER_DK].astype(bf16)
        s1 = _dot(sk_ref[0, hd], q1, NT)
        s2 = _dot(sk_ref[1, hd], q2, NT)
        v1 = _top_sorted(s1, PEER_TOPK)
        v2 = _top_sorted(s2, PEER_TOPK)
        cands = [v1[a] + v2[b] for a, b in _PAIR_CANDS]
        pad = -(-len(cands) // SUBLANES) * SUBLANES - len(cands)
        cmat = jnp.concatenate(cands + [jnp.full_like(cands[0], -jnp.inf)] * pad, axis=0)
        sv = _top_sorted(cmat, PEER_TOPK)
        tau = sv[PEER_TOPK - 1]
        zsum = jnp.zeros_like(tau)
        for (a, b), c in zip(_PAIR_CANDS, cands):
            zsum = zsum + jnp.where(c >= tau, jnp.exp(v1[a] - v1[0]) * jnp.exp(v2[b] - v2[0]), 0.0)
        s1_ref[:, hd, :] = s1
        c1_ref[:, hd, :] = jnp.exp(s1 - v1[0]) / zsum
        s2_ref[hd] = s2
        e2_ref[hd] = jnp.exp(s2 - v2[0])
        tau_ref[hd:hd + 1, :] = tau


def _peer_router(h, g, wq, subkeys):
    n, d = h.shape
    tm = _lane_block(n, 256)
    hk = (PEER_KEYS, PEER_HEADS, n)
    kh = (PEER_HEADS, PEER_KEYS, n)
    return pl.pallas_call(
        _peer_router_kernel,
        grid=(n // tm,),
        in_specs=[pl.BlockSpec((tm, d), lambda i: (i, 0)), _const_spec((1, d)), _const_spec(wq.shape),
                  _const_spec(subkeys.shape)],
        out_specs=[pl.BlockSpec((d, tm), lambda i: (0, i)),
                   pl.BlockSpec((PEER_KEYS, PEER_HEADS, tm), lambda i: (0, 0, i)),
                   pl.BlockSpec((PEER_KEYS, PEER_HEADS, tm), lambda i: (0, 0, i)),
                   pl.BlockSpec((PEER_HEADS, PEER_KEYS, tm), lambda i: (0, 0, i)),
                   pl.BlockSpec((PEER_HEADS, PEER_KEYS, tm), lambda i: (0, 0, i)),
                   pl.BlockSpec((PEER_HEADS, tm), lambda i: (0, i))],
        out_shape=[jax.ShapeDtypeStruct((d, n), bf16), jax.ShapeDtypeStruct(hk, f32), jax.ShapeDtypeStruct(hk, f32),
                   jax.ShapeDtypeStruct(kh, f32), jax.ShapeDtypeStruct(kh, f32),
                   jax.ShapeDtypeStruct((PEER_HEADS, n), f32)],
        compiler_params=_params(("parallel",)),
        name="peer_router",
    )(h, g.reshape(1, d), wq, subkeys)


PEER_ET = 512
PEER_TT = 768


def _peer_dense_kernel(xt_ref, u_ref, v_ref, s1_ref, c1_ref, s2_ref, e2_ref, tau_ref, o_ref, pre_ref, gwt_ref, *, tt):
    j = pl.program_id(1)

    @pl.when(j == 0)
    def _():
        o_ref[...] = jnp.zeros_like(o_ref)

    pre_ref[...] = _dot(u_ref[...], xt_ref[...])
    n1 = PEER_ET // PEER_KEYS
    for tc in range(tt // LANES):
        ls = slice(tc * LANES, (tc + 1) * LANES)
        for a in range(n1):
            acc = jnp.zeros((PEER_KEYS, LANES), f32)
            for hd in range(PEER_HEADS):
                c = s2_ref[hd, :, ls] + s1_ref[a, hd:hd + 1, ls]
                sel = c >= tau_ref[hd:hd + 1, ls]
                acc = acc + jnp.where(sel, e2_ref[hd, :, ls], 0.0) * c1_ref[a, hd:hd + 1, ls]
            pa = pre_ref[a * PEER_KEYS:(a + 1) * PEER_KEYS, ls]
            act = 0.5 * pa * (1.0 + lax.erf(pa * (2.0 ** -0.5)))
            gwt_ref[ls, a * PEER_KEYS:(a + 1) * PEER_KEYS] = (acc * act).T.astype(gwt_ref.dtype)
    o_ref[...] += _dot(gwt_ref[...], v_ref[...])


def _peer_dense(xt, u, v, s1, c1, s2, e2, tau):
    d, n = xt.shape
    e = u.shape[0]
    tt = _lane_block(n, PEER_TT)
    n1 = PEER_ET // PEER_KEYS
    return pl.pallas_call(
        functools.partial(_peer_dense_kernel, tt=tt),
        grid=(n // tt, e // PEER_ET),
        in_specs=[pl.BlockSpec((d, tt), lambda i, j: (0, i)),
                  pl.BlockSpec((PEER_ET, d), lambda i, j: (j, 0)),
                  pl.BlockSpec((PEER_ET, d), lambda i, j: (j, 0)),
                  pl.BlockSpec((n1, PEER_HEADS, tt), lambda i, j: (j, 0, i)),
                  pl.BlockSpec((n1, PEER_HEADS, tt), lambda i, j: (j, 0, i)),
                  pl.BlockSpec((PEER_HEADS, PEER_KEYS, tt), lambda i, j: (0, 0, i)),
                  pl.BlockSpec((PEER_HEADS, PEER_KEYS, tt), lambda i, j: (0, 0, i)),
                  pl.BlockSpec((PEER_HEADS, tt), lambda i, j: (0, i))],
        out_specs=pl.BlockSpec((tt, d), lambda i, j: (i, 0)),
        out_shape=jax.ShapeDtypeStruct((n, d), f32),
        scratch_shapes=[pltpu.VMEM((PEER_ET, tt), f32), pltpu.VMEM((tt, PEER_ET), bf16)],
        compiler_params=_params(("parallel", "arbitrary")),
        name="peer_dense",
    )(xt, u, v, s1, c1, s2, e2, tau)


def _final_kernel(h_ref, p_ref, g_ref, o_ref):
    x = h_ref[...] + p_ref[...]
    ms = jnp.mean(x * x, axis=-1, keepdims=True)
    o_ref[...] = x * lax.rsqrt(ms + NORM_EPS) * g_ref[...]


def _final(h, p, g):
    n, d = h.shape
    tm = _block(n, 512)
    row = pl.BlockSpec((tm, d), lambda i: (i, 0))
    return pl.pallas_call(
        _final_kernel,
        grid=(n // tm,),
        in_specs=[row, row, _const_spec((1, d))],
        out_specs=row,
        out_shape=jax.ShapeDtypeStruct((n, d), f32),
        compiler_params=_params(("parallel",)),
        name="final_norm",
    )(h, p, g.reshape(1, d))


def _split_w_in(w_in, d):
    in_sizes = (ATT_W, KV_W, KV_W, IDX_W, IDX_DIM, IDX_HEADS, d, d, RWKV_IN_W)
    q, k, v, qi, ki, wi, ga, gb, zr = jnp.split(w_in, [int(c) for c in np.cumsum(in_sizes)[:-1]], axis=1)
    r_, k_, v_, zw, za, zg = jnp.split(zr, [int(c) for c in np.cumsum(RWKV_SIZES)[:-1]], axis=1)
    z = lambda n: jnp.zeros((d, n), w_in.dtype)
    w_qkv = jnp.concatenate([q, k, v, qi, ki, wi, z(LANES - IDX_DIM - IDX_HEADS)], axis=1)
    w_gate = jnp.concatenate([ga, gb], axis=1)
    w_rw = jnp.concatenate([r_, k_, v_, zw, z(LANES - DECAY_LORA), za, z(LANES - AAA_LORA), zg], axis=1)
    return w_qkv.astype(bf16), w_gate.astype(bf16), w_rw.astype(bf16)


def _unpad_rw(zr_pad):
    return jnp.concatenate([zr_pad[..., :3 * RWKV_W], zr_pad[..., O_ZW:O_ZW + DECAY_LORA],
                            zr_pad[..., O_ZA:O_ZA + AAA_LORA], zr_pad[..., O_ZG:]], axis=-1)


def _pad_rw(zr):
    cuts = [int(c) for c in np.cumsum(RWKV_SIZES)[:-1]]
    r_, k_, v_, zw, za, zg = jnp.split(zr, cuts, axis=-1)
    z = lambda n: jnp.zeros(zr.shape[:-1] + (n,), zr.dtype)
    return jnp.concatenate([r_, k_, v_, zw, z(LANES - DECAY_LORA), za, z(LANES - AAA_LORA), zg], axis=-1)


def kernel(x_prompt, x_sample, cache_k, cache_v, cache_idx_k, state_wkv, state_shift, norm1_g, w_in, rwkv_mu, rwkv_w0,
           rwkv_w2, rwkv_a0, rwkv_a2, rwkv_g2, rwkv_k_k, rwkv_k_a, rwkv_r_k, rwkv_ln_w, rwkv_ln_b, attn_up, rwkv_up,
           w_out, norm2_g, peer_wq, peer_subkeys, peer_u, peer_v, final_g):
    bp, tp, d = x_prompt.shape
    bs, ts, _ = x_sample.shape
    past = cache_k.shape[2]
    n_p, n_s = bp * tp, bs * ts
    n_all = n_p + n_s
    assert tp % DSA_KB == 0 and n_s % DSA_KB == 0 and ts <= DSA_QB
    l = 0

    xp, xs = x_prompt.reshape(n_p, d), x_sample.reshape(n_s, d)
    pos_p = jnp.arange(tp, dtype=i32)
    pos_s = past + jnp.arange(ts, dtype=i32)
    pos = jnp.concatenate([jnp.tile(pos_p, bp), jnp.tile(pos_s, bs)])

    w_qkv, w_gate, w_rw = _split_w_in(w_in[l], d)
    xn = jnp.concatenate([_rmsnorm(xp, norm1_g[l], bf16), _rmsnorm(xs, norm1_g[l], bf16)], axis=0)
    z_qkv = _matmul(xn, w_qkv, 1024, 640)
    gates = _matmul(xn, w_gate, 1024, 1024)
    zr = _matmul(xn, w_rw, 1024, 896)

    ca, sa, cb, sb = _rope_tables(pos)
    q, k_f, k_b, v_f, vt, qi, ki_f, kilo, kihi, wi = _rope(z_qkv, ca, sa, cb, sb)

    nqb_p = tp // DSA_QB
    topk_p = min(IDX_TOPK, tp // 4)
    wit_p = jnp.transpose(wi[:n_p].reshape(bp * nqb_p, DSA_QB, IDX_HEADS), (0, 2, 1))
    att_p = _dsa(q, qi, wit_p, pos_p.reshape(nqb_p, 1, DSA_QB), k_b, vt, kilo, kihi, bp, tp, tp, topk_p)

    rs = lambda a: a[n_p:].reshape(bs, ts, a.shape[-1])
    ls = past + ts
    lp_s = -(-ls // DSA_KB) * DSA_KB
    topk_s = min(IDX_TOPK, ls // 4)
    padq = lambda a: jnp.pad(a, ((0, 0), (0, DSA_QB - ts), (0, 0)))
    keys = lambda c, new: jnp.pad(jnp.concatenate([c, new], axis=1), ((0, 0), (0, lp_s - ls), (0, 0)))
    flat = lambda a: a.reshape(-1, a.shape[-1])
    cki = cache_idx_k[l]
    zeros_ki = jnp.zeros_like(cki)
    k_s = keys(cache_k[l].reshape(bs, past, KV_W).astype(bf16), rs(k_b))
    v_s = keys(cache_v[l].reshape(bs, past, KV_W).astype(bf16), rs(v_f).astype(bf16))
    vt_s = jnp.transpose(v_s.reshape(bs * (lp_s // DSA_KB), DSA_KB, KV_W), (0, 2, 1))
    kilo_s = keys(jnp.concatenate([cki, zeros_ki], axis=-1).astype(bf16), rs(kilo))
    kihi_s = keys(jnp.concatenate([zeros_ki, cki], axis=-1).astype(bf16), rs(kihi))
    qpos_s = jnp.concatenate([pos_s, jnp.full((DSA_QB - ts,), past, i32)]).reshape(1, 1, DSA_QB)
    att_s = _dsa(flat(padq(rs(q))), flat(padq(rs(qi))), jnp.transpose(padq(rs(wi)), (0, 2, 1)), qpos_s,
                 flat(k_s), vt_s, flat(kilo_s), flat(kihi_s), bs, lp_s, ls, topk_s)
    att_s = att_s.reshape(bs, DSA_QB, ATT_W)[:, :ts].reshape(n_s, ATT_W)

    bd = _head_blockdiag()
    rw_w = _rwkv_weights(rwkv_mu[l], rwkv_w0[l], rwkv_w2[l], rwkv_a0[l], rwkv_a2[l], rwkv_g2[l], rwkv_k_k[l],
                         rwkv_k_a[l], rwkv_r_k[l].reshape(-1), bd)
    zr_s = zr[n_p:].reshape(bs, ts, RW_PAD_W)
    prev_s = jnp.concatenate([_pad_rw(state_shift[l]), zr_s[:, :-1]], axis=1).reshape(n_s, RW_PAD_W)
    prep_p = _rwkv_prep(zr, None, n_p, tp, rw_w)
    prep_s = _rwkv_prep(zr_s.reshape(n_s, RW_PAD_W), prev_s, n_s, None, rw_w)
    seq_p = [a.reshape(bp, tp, RWKV_W) for a in prep_p[:6]]
    y_p, st_p = _wkv(*seq_p, jnp.zeros((bp, RWKV_HEADS, RWKV_HEAD, RWKV_HEAD), f32))
    ts_pad = -(-ts // WKV_C) * WKV_C
    pad_t = lambda a: jnp.pad(a.reshape(bs, ts, RWKV_W), ((0, 0), (0, ts_pad - ts), (0, 0)))
    y_s, st_s = _wkv(*[pad_t(a) for a in prep_s[:6]], state_wkv[l])

    mix_w = (rwkv_ln_w[l], rwkv_ln_b[l], bd, attn_up[l].astype(bf16), rwkv_up[l].astype(bf16), w_out[l].astype(bf16))
    h_p = _mix(att_p, y_p.reshape(n_p, RWKV_W), prep_p[7], prep_p[6], gates, 0, xp, *mix_w)
    h_s = _mix(att_s, y_s[:, :ts].reshape(n_s, RWKV_W), prep_s[7], prep_s[6], gates, n_p, xs, *mix_w)
    h = jnp.concatenate([h_p, h_s], axis=0)

    n_pad = -(-n_all // LANES) * LANES
    h_pad = jnp.pad(h, ((0, n_pad - n_all), (0, 0))) if n_pad != n_all else h
    xt, s1, c1, s2, e2, tau = _peer_router(h_pad, norm2_g[l], peer_wq[l].astype(bf16), peer_subkeys[l].astype(bf16))
    peer = _peer_dense(xt, peer_u[l].astype(bf16), peer_v[l].astype(bf16), s1, c1, s2, e2, tau)
    y_all = _final(h_pad, peer, final_g)[:n_all]

    y_prompt = y_all[:n_p].reshape(bp, tp, d)
    y_sample = y_all[n_p:].reshape(bs, ts, d)
    kv = lambda a, b_, t: a.reshape(1, b_, t, N_KV_HEADS, HEAD_DIM)
    shift_p = _unpad_rw(zr[:n_p].reshape(bp, tp, RW_PAD_W)[:, -1:])[None]
    shift_s = _unpad_rw(zr_s[:, -1:])[None]
    return (y_prompt, y_sample,
            kv(k_f[:n_p], bp, tp), kv(v_f[:n_p], bp, tp), ki_f[:n_p].reshape(1, bp, tp, IDX_DIM),
            st_p.reshape(1, bp, RWKV_HEADS, RWKV_HEAD, RWKV_HEAD), shift_p,
            kv(k_f[n_p:], bs, ts), kv(v_f[n_p:], bs, ts), ki_f[n_p:].reshape(1, bs, ts, IDX_DIM),
            st_s.reshape(1, bs, RWKV_HEADS, RWKV_HEAD, RWKV_HEAD), shift_s)
```
